```python
import jax, jax.numpy as jnp
from jax import lax
import numpy as np

D_MODEL = 1024
BATCH = 16
SEQ = 2048
DEPTH = 2

CHUNK = 64
MIX_WIDTH = D_MODEL
RWKV_WIDTH = MIX_WIDTH // 2
CONV_WIDTH = MIX_WIDTH - RWKV_WIDTH
RWKV_HEAD = 64
RWKV_HEADS = RWKV_WIDTH // RWKV_HEAD
DECAY_LORA = 64
ICLR_LORA = 64
GATE_LORA = 128
CONV_K = 31
D_FF = 4 * D_MODEL
NORM_EPS = 1e-5
LN_EPS = 1e-5
GN_EPS = 64e-5
L2_EPS = 1e-12

OFF_R = 0
OFF_K = OFF_R + RWKV_WIDTH
OFF_V = OFF_K + RWKV_WIDTH
OFF_WD = OFF_V + RWKV_WIDTH
OFF_AD = OFF_WD + DECAY_LORA
OFF_GD = OFF_AD + ICLR_LORA
RWKV_COLS = OFF_GD + GATE_LORA
CONV_COLS = 2 * CONV_WIDTH
IN_COLS = RWKV_COLS + CONV_COLS

kernel_name = "hybrid_rwkv7_conformer_conv_block"


def rmsnorm(x, g):
    xf = x.astype(jnp.float32)
    y = xf * lax.rsqrt(jnp.mean(xf * xf, axis=-1, keepdims=True) + NORM_EPS)
    return (y * g.astype(jnp.float32)).astype(x.dtype)


def layernorm(x, g, b):
    xf = x.astype(jnp.float32)
    mu = jnp.mean(xf, axis=-1, keepdims=True)
    xc = xf - mu
    var = jnp.mean(xc * xc, axis=-1, keepdims=True)
    y = xc * lax.rsqrt(var + LN_EPS) * g.astype(jnp.float32) + b.astype(jnp.float32)
    return y.astype(x.dtype)


def token_shift(p):
    return jnp.pad(p[:, :-1], ((0, 0), (1, 0), (0, 0)))


def rwkv7_recurrence(r, w, k, v, a, b):
    B, T, H, N = r.shape
    nc = T // CHUNK

    def to_chunks(t):
        return jnp.moveaxis(t.astype(jnp.float32), 1, 0).reshape(nc, CHUNK, B, H, N)

    xs = tuple(to_chunks(t) for t in (r, w, k, v, a, b))

    def step(S, inp):
        r_t, w_t, k_t, v_t, a_t, b_t = inp
        Sa = jnp.einsum('bhij,bhj->bhi', S, a_t)
        S = (S * w_t[:, :, None, :]
             + Sa[..., None] * b_t[:, :, None, :]
             + v_t[..., None] * k_t[:, :, None, :])
        y_t = jnp.einsum('bhij,bhj->bhi', S, r_t)
        return S, y_t

    def chunk_step(S, chunk_inp):
        return lax.scan(step, S, chunk_inp)

    S0 = jnp.zeros((B, H, N, N), jnp.float32)
    _, ys = lax.scan(chunk_step, S0, xs)
    return jnp.moveaxis(ys.reshape(T, B, H, N), 0, 1)


def hybrid_mixer(h, w_in, mu_shift, w0, w_up, a0, a_up, g_up, k_k, k_a, r_k,
                 gn_g, gn_b, dw_w, dw_b, cln_g, cln_b, w_out):
    B, T, _ = h.shape
    dt = h.dtype
    p = h @ w_in
    pr = p[..., :RWKV_COLS]
    pc = p[..., RWKV_COLS:]

    pr = pr + (token_shift(pr) - pr) * mu_shift
    r = pr[..., OFF_R:OFF_K]
    k = pr[..., OFF_K:OFF_V]
    v = pr[..., OFF_V:OFF_WD]
    wd = pr[..., OFF_WD:OFF_AD]
    ad = pr[..., OFF_AD:OFF_GD]
    gd = pr[..., OFF_GD:RWKV_COLS]

    w_log = -jax.nn.softplus(-(w0 + jnp.tanh(wd) @ w_up)) - 0.5
    decay = jnp.exp(-jnp.exp(w_log.astype(jnp.float32)))
    a = jax.nn.sigmoid(a0 + ad @ a_up)
    g = jax.nn.sigmoid(gd) @ g_up

    hs = (B, T, RWKV_HEADS, RWKV_HEAD)
    kk = (k * k_k).reshape(hs).astype(jnp.float32)
    kk = kk / jnp.maximum(jnp.linalg.norm(kk, axis=-1, keepdims=True), L2_EPS)
    k = k * (1.0 + (a - 1.0) * k_a)

    r_h = r.reshape(hs)
    k_h = k.reshape(hs)
    v_h = v.reshape(hs)
    a_h = a.reshape(hs).astype(jnp.float32)
    y = rwkv7_recurrence(r_h, decay.reshape(hs), k_h, v_h, -kk, kk * a_h)

    mu = jnp.mean(y, axis=-1, keepdims=True)
    yc = y - mu
    y = yc * lax.rsqrt(jnp.mean(yc * yc, axis=-1, keepdims=True) + GN_EPS)
    y = y.reshape(B, T, RWKV_WIDTH) * gn_g.astype(jnp.float32) + gn_b.astype(jnp.float32)
    bonus = (jnp.sum(r_h * k_h * r_k, axis=-1, keepdims=True) * v_h).reshape(B, T, RWKV_WIDTH)
    y_rwkv = ((y.astype(dt) + bonus) * g).astype(dt)

    u = pc[..., :CONV_WIDTH] * jax.nn.sigmoid(pc[..., CONV_WIDTH:])
    u = lax.conv_general_dilated(
        u, dw_w[:, None, :].astype(u.dtype), window_strides=(1,),
        padding=[(CONV_K - 1, 0)], dimension_numbers=('NWC', 'WIO', 'NWC'),
        feature_group_count=CONV_WIDTH) + dw_b
    u = jax.nn.silu(layernorm(u, cln_g, cln_b))

    mixed = jnp.concatenate([y_rwkv, u.astype(dt)], axis=-1)
    return mixed @ w_out


def squared_relu_mlp(h, w1, w2):
    return jnp.square(jax.nn.relu(h @ w1)) @ w2


def setup_inputs(seed: int = 0) -> dict:
    key = jax.random.key(seed)
    ks = jax.random.split(key, 32)
    f32 = jnp.float32
    L = DEPTH

    def nrm(k, shape, scale):
        return jax.random.normal(k, shape, f32) * scale

    decay_base = jnp.linspace(-6.0, -1.0, RWKV_WIDTH, dtype=f32)
    return {
        "x": nrm(ks[0], (BATCH, SEQ, D_MODEL), 1.0),
        "norm1_g": 1.0 + nrm(ks[1], (L, D_MODEL), 0.02),
        "w_in": nrm(ks[2], (L, D_MODEL, IN_COLS), D_MODEL ** -0.5),
        "mu_shift": jax.random.uniform(ks[3], (L, RWKV_COLS), f32),
        "w0": decay_base[None, :] + nrm(ks[4], (L, RWKV_WIDTH), 0.1),
        "w_up": nrm(ks[5], (L, DECAY_LORA, RWKV_WIDTH), 0.5 * DECAY_LORA ** -0.5),
        "a0": nrm(ks[6], (L, RWKV_WIDTH), 0.1),
        "a_up": nrm(ks[7], (L, ICLR_LORA, RWKV_WIDTH), 0.5 * ICLR_LORA ** -0.5),
        "g_up": nrm(ks[8], (L, GATE_LORA, RWKV_WIDTH), GATE_LORA ** -0.5),
        "k_k": 0.85 + nrm(ks[9], (L, RWKV_WIDTH), 0.02),
        "k_a": 1.0 + nrm(ks[10], (L, RWKV_WIDTH), 0.02),
        "r_k": -0.04 + nrm(ks[11], (L, RWKV_HEADS, RWKV_HEAD), 0.1),
        "gn_g": 1.0 + nrm(ks[12], (L, RWKV_WIDTH), 0.02),
        "gn_b": nrm(ks[13], (L, RWKV_WIDTH), 0.02),
        "dw_w": nrm(ks[14], (L, CONV_K, CONV_WIDTH), CONV_K ** -0.5),
        "dw_b": nrm(ks[15], (L, CONV_WIDTH), 0.02),
        "cln_g": 1.0 + nrm(ks[16], (L, CONV_WIDTH), 0.02),
        "cln_b": nrm(ks[17], (L, CONV_WIDTH), 0.02),
        "w_out": nrm(ks[18], (L, MIX_WIDTH, D_MODEL), MIX_WIDTH ** -0.5),
        "norm2_g": 1.0 + nrm(ks[19], (L, D_MODEL), 0.02),
        "w_ff1": nrm(ks[20], (L, D_MODEL, D_FF), D_MODEL ** -0.5),
        "w_ff2": nrm(ks[21], (L, D_FF, D_MODEL), D_FF ** -0.5),
        "final_g": 1.0 + nrm(ks[22], (D_MODEL,), 0.02),
    }


def reference(x, norm1_g, w_in, mu_shift, w0, w_up, a0, a_up, g_up, k_k, k_a, r_k,
              gn_g, gn_b, dw_w, dw_b, cln_g, cln_b, w_out, norm2_g, w_ff1, w_ff2,
              final_g):
    for l in range(DEPTH):
        h = rmsnorm(x, norm1_g[l])
        x = x + hybrid_mixer(h, w_in[l], mu_shift[l], w0[l], w_up[l], a0[l], a_up[l],
                             g_up[l], k_k[l], k_a[l], r_k[l], gn_g[l], gn_b[l],
                             dw_w[l], dw_b[l], cln_g[l], cln_b[l], w_out[l])
        h = rmsnorm(x, norm2_g[l])
        x = x + squared_relu_mlp(h, w_ff1[l], w_ff2[l])
    return rmsnorm(x, final_g)
```

```python
import functools

import jax
import jax.numpy as jnp
from jax import lax
from jax.experimental import pallas as pl
from jax.experimental.pallas import tpu as pltpu

F32 = jnp.float32
BF16 = jnp.bfloat16

D_MODEL = 1024
RW = 512
HEAD = 64
CW = 512
DECAY_LORA = 64
ICLR_LORA = 64
GATE_LORA = 128
CONV_K = 31
D_FF = 4 * D_MODEL
NORM_EPS = 1e-5
LN_EPS = 1e-5
GN_EPS = 64e-5
L2_EPS = 1e-12

OFF_K = RW
OFF_V = 2 * RW
OFF_WD = 3 * RW
OFF_GD = OFF_WD + DECAY_LORA + ICLR_LORA
RWKV_COLS = OFF_GD + GATE_LORA
IN_COLS = RWKV_COLS + 2 * CW

CHUNK = 64
GROUP = 256
N_GROUPS = RW // GROUP
HEADS_PER_GROUP = GROUP // HEAD
CONV_HIST = 32

ROW_TILE = 512
MIX_TILE = 256
FF_SPLIT = 4
VMEM_LIMIT = 56 * 1024 * 1024

NN = ((1,), (0,))
NT = ((1,), (1,))


def _dot(a, b, dims=NN):
    return lax.dot_general(a, b, (dims, ((), ())), preferred_element_type=F32)


def _split(x):
    hi = x.astype(BF16)
    lo = (x - hi.astype(F32)).astype(BF16)
    return hi, lo


def _mm3(a, b, dims=NN):
    ah, al = a
    bh, bl = b
    return _dot(ah, bh, dims) + (_dot(ah, bl, dims) + _dot(al, bh, dims))


def _rmsnorm(x, g):
    ms = jnp.mean(x * x, axis=-1, keepdims=True)
    return x * lax.rsqrt(ms + NORM_EPS) * g


def _inproj_kernel(x_ref, g_ref, w_ref, p_ref):
    h = _rmsnorm(x_ref[...], g_ref[...]).astype(BF16)
    p_ref[...] = _dot(h, w_ref[...])


def _const_spec(shape):
    return pl.BlockSpec(shape, lambda *_: (0,) * len(shape), pipeline_mode=pl.Buffered(1))


def _in_proj(x2, g, w_bf):
    m = x2.shape[0]
    return pl.pallas_call(
        _inproj_kernel,
        grid=(m // ROW_TILE,),
        in_specs=[
            pl.BlockSpec((ROW_TILE, D_MODEL), lambda i: (i, 0)),
            _const_spec((1, D_MODEL)),
            _const_spec((D_MODEL, IN_COLS)),
        ],
        out_specs=pl.BlockSpec((ROW_TILE, IN_COLS), lambda i: (i, 0)),
        out_shape=jax.ShapeDtypeStruct((m, IN_COLS), F32),
        compiler_params=pltpu.CompilerParams(
            dimension_semantics=("arbitrary",), vmem_limit_bytes=VMEM_LIMIT),
        name="in_proj",
    )(x2, g, w_bf)


def _outffn_kernel(x_ref, mix_ref, wout_ref, g2_ref, w1_ref, w2_ref, gf_ref, o_ref, *, final):
    x1 = x_ref[...] + _dot(mix_ref[...], wout_ref[...])
    h2 = _rmsnorm(x1, g2_ref[...]).astype(BF16)
    acc = x1
    blk = D_FF // FF_SPLIT
    for j in range(FF_SPLIT):
        hid = _dot(h2, w1_ref[:, j * blk:(j + 1) * blk])
        hid = jnp.square(jnp.maximum(hid, 0.0)).astype(BF16)
        acc = acc + _dot(hid, w2_ref[j * blk:(j + 1) * blk, :])
    if final:
        acc = _rmsnorm(acc, gf_ref[...])
    o_ref[...] = acc


def _out_ffn(x2, mixed, wout_bf, g2, w1_bf, w2_bf, gf, final):
    m = x2.shape[0]
    return pl.pallas_call(
        functools.partial(_outffn_kernel, final=final),
        grid=(m // ROW_TILE,),
        in_specs=[
            pl.BlockSpec((ROW_TILE, D_MODEL), lambda i: (i, 0)),
            pl.BlockSpec((ROW_TILE, D_MODEL), lambda i: (i, 0)),
            _const_spec((D_MODEL, D_MODEL)),
            _const_spec((1, D_MODEL)),
            _const_spec((D_MODEL, D_FF)),
            _const_spec((D_FF, D_MODEL)),
            _const_spec((1, D_MODEL)),
        ],
        out_specs=pl.BlockSpec((ROW_TILE, D_MODEL), lambda i: (i, 0)),
        out_shape=jax.ShapeDtypeStruct((m, D_MODEL), F32),
        compiler_params=pltpu.CompilerParams(
            dimension_semantics=("arbitrary",), vmem_limit_bytes=VMEM_LIMIT),
        name="out_ffn",
    )(x2, mixed, wout_bf, g2, w1_bf, w2_bf, gf)


def _head_sum(x, ones_bd):
    parts = []
    for g in range(N_GROUPS):
        hi, lo = _split(x[:, g * GROUP:(g + 1) * GROUP])
        parts.append(_dot(hi, ones_bd) + _dot(lo, ones_bd))
    return jnp.concatenate(parts, axis=-1)


def _block_diag(q, bd_mask):
    return jnp.concatenate([q] * HEADS_PER_GROUP, axis=0) * bd_mask


def _bd_pair(pair, bd_mask):
    return _block_diag(pair[0], bd_mask), _block_diag(pair[1], bd_mask)


def _unit_lower_inverse(n_l, row, col, bd_mask):
    diff = row ^ col
    x = jnp.where(diff == 0, 1.0, 0.0) + jnp.where(diff == 1, n_l, 0.0)
    for lvl in range(1, 6):
        n_s = jnp.where((diff >> lvl) == 1, n_l, 0.0)
        xs = _split(x)
        y = _mm3(xs, _bd_pair(_split(n_s), bd_mask))
        x = x + _mm3(_split(y), _bd_pair(xs, bd_mask))
    return x


def _mixer_kernel(p_ref, mu_ref, w0_ref, a0_ref, wa_ref, gup_ref, kk_ref, ka_ref, rk_ref,
                  gng_ref, gnb_ref, dww_ref, dwb_ref, clg_ref, clb_ref, ones_ref, ltri_ref,
                  o_ref,
                  s_ref, carry_ref, conv_ref, r_s, k_s, v_s, kk_s, b_s, lw_s, y_s):
    tt = p_ref.shape[0]
    t_idx = pl.program_id(1)

    @pl.when(t_idx == 0)
    def _():
        s_ref[...] = jnp.zeros_like(s_ref)
        carry_ref[...] = jnp.zeros_like(carry_ref)
        conv_ref[0:CONV_HIST, :] = jnp.zeros((CONV_HIST, CW), F32)

    ones_bd = ones_ref[...]
    mu = mu_ref[...]
    row0 = lax.broadcasted_iota(jnp.int32, (tt, 1), 0) == 0

    def shifted(lo, hi):
        cur = p_ref[:, lo:hi]
        prev = pltpu.roll(cur, 1, axis=0)
        prev = jnp.where(row0, carry_ref[:, lo:hi], prev)
        return cur + (prev - cur) * mu[:, lo:hi]

    r = shifted(0, OFF_K)
    k = shifted(OFF_K, OFF_V)
    v = shifted(OFF_V, OFF_WD)
    wa = shifted(OFF_WD, OFF_GD)
    gd = shifted(OFF_GD, RWKV_COLS)
    carry_ref[...] = p_ref[tt - 1:tt, 0:RWKV_COLS]

    lane = lax.broadcasted_iota(jnp.int32, wa.shape, 1)
    wa = jnp.where(lane < DECAY_LORA, jnp.tanh(wa), wa)
    lora = _dot(wa.astype(BF16), wa_ref[...])
    z = w0_ref[...] + lora[:, :RW]
    w_log = -(jnp.maximum(-z, 0.0) + jnp.log1p(jnp.exp(-jnp.abs(z)))) - 0.5
    lw_s[...] = -jnp.exp(w_log)
    a = jax.nn.sigmoid(a0_ref[...] + lora[:, RW:])
    gate = _dot(jax.nn.sigmoid(gd).astype(BF16), gup_ref[...])

    kk = k * kk_ref[...]
    inv_norm = jnp.minimum(lax.rsqrt(_head_sum(kk * kk, ones_bd)), 1.0 / L2_EPS)
    kk = kk * inv_norm
    k = k * (1.0 + (a - 1.0) * ka_ref[...])
    bonus = _head_sum(r * k * rk_ref[...], ones_bd) * v
    r_s[...] = r
    k_s[...] = k
    v_s[...] = v
    kk_s[...] = kk
    b_s[...] = kk * a

    bd_mask = ones_bd
    ltri = ltri_ref[...]
    row = lax.broadcasted_iota(jnp.int32, (CHUNK, GROUP), 0)
    col = lax.broadcasted_iota(jnp.int32, (CHUNK, GROUP), 1) & (HEAD - 1)
    strict = col < row
    incl = col <= row

    def chunk_body(c, carry):
        rows = pl.ds(pl.multiple_of(c * CHUNK, CHUNK), CHUNK)
        lw = lw_s[rows, :]
        lw_hi, lw_lo = _split(lw)
        cum = _dot(ltri, lw_hi) + _dot(ltri, lw_lo)
        cum_end = cum[CHUNK - 1:CHUNK, :]
        e_pos = jnp.exp(cum)
        e_prev = jnp.exp(cum - lw)
        e_neg = jnp.exp(-cum)
        e_rem = jnp.exp(cum_end - cum)
        g_end = jnp.exp(cum_end)
        r_c = r_s[rows, :]
        k_c = k_s[rows, :]
        v_c = v_s[rows, :]
        b_c = b_s[rows, :]
        r_t = r_c * e_pos
        a_t = -kk_s[rows, :] * e_prev
        k_t = k_c * e_neg
        b_t = b_c * e_neg
        k_h = k_c * e_rem
        b_h = b_c * e_rem
        y_parts = []
        for g in range(N_GROUPS):
            sl = slice(g * GROUP, (g + 1) * GROUP)
            lhs = _split(jnp.concatenate([a_t[:, sl], r_t[:, sl]], axis=0))
            ab = _mm3(lhs, _bd_pair(_split(b_t[:, sl]), bd_mask), NT)
            ak = _mm3(lhs, _bd_pair(_split(k_t[:, sl]), bd_mask), NT)
            n_l = jnp.where(strict, ab[:CHUNK], 0.0)
            a_ak = jnp.where(strict, ak[:CHUNK], 0.0)
            a_rb = jnp.where(incl, ab[CHUNK:], 0.0)
            a_rk = jnp.where(incl, ak[CHUNK:], 0.0)
            t_inv = _unit_lower_inverse(n_l, row, col, bd_mask)

            s0 = s_ref[g]
            s0_p = _split(s0)
            v_g = v_c[:, sl]
            v_bd = _bd_pair(_split(v_g), bd_mask)
            w = _mm3((lhs[0][:CHUNK], lhs[1][:CHUNK]), s0_p, NT) + _mm3(_split(a_ak), v_bd)
            u = _mm3(_split(t_inv), _bd_pair(_split(w), bd_mask))
            y = (_mm3((lhs[0][CHUNK:], lhs[1][CHUNK:]), s0_p, NT)
                 + _mm3(_split(a_rb), _bd_pair(_split(u), bd_mask))
                 + _mm3(_split(a_rk), v_bd))
            y_parts.append(y)
            uv_t = jnp.concatenate([u, v_g], axis=0).T
            bk = jnp.concatenate([b_h[:, sl], k_h[:, sl]], axis=0)
            s_new = s0 * g_end[:, sl] + _mm3(_split(uv_t), _split(bk))
            s_ref[g] = jnp.where(bd_mask > 0, s_new, 0.0)
        y_s[rows, :] = jnp.concatenate(y_parts, axis=-1)
        return carry

    lax.fori_loop(0, tt // CHUNK, chunk_body, 0)

    y = y_s[...]
    mean = _head_sum(y, ones_bd) * (1.0 / HEAD)
    yc = y - mean
    var = _head_sum(yc * yc, ones_bd) * (1.0 / HEAD)
    yn = yc * lax.rsqrt(var + GN_EPS) * gng_ref[...] + gnb_ref[...]
    o_ref[:, 0:RW] = ((yn + bonus) * gate).astype(o_ref.dtype)

    u = p_ref[:, RWKV_COLS:RWKV_COLS + CW] * jax.nn.sigmoid(p_ref[:, RWKV_COLS + CW:IN_COLS])
    conv_ref[CONV_HIST:CONV_HIST + tt, :] = u
    acc = jnp.zeros((tt, CW), F32) + dwb_ref[...]
    base = CONV_HIST - (CONV_K - 1)
    for j in range(CONV_K):
        acc = acc + conv_ref[base + j:base + j + tt, :] * dww_ref[j:j + 1, :]
    conv_ref[0:CONV_HIST, :] = conv_ref[tt:tt + CONV_HIST, :]
    mu_c = jnp.mean(acc, axis=-1, keepdims=True)
    xc = acc - mu_c
    var_c = jnp.mean(xc * xc, axis=-1, keepdims=True)
    ln = xc * lax.rsqrt(var_c + LN_EPS) * clg_ref[...] + clb_ref[...]
    o_ref[:, RW:RW + CW] = (ln * jax.nn.sigmoid(ln)).astype(o_ref.dtype)


def _mixer(p, batch, seq, params):
    tt = min(MIX_TILE, seq)
    n_t = seq // tt
    consts = [_const_spec(a.shape) for a in params]
    return pl.pallas_call(
        _mixer_kernel,
        grid=(batch, n_t),
        in_specs=[pl.BlockSpec((tt, IN_COLS), lambda b, t: (b * n_t + t, 0))] + consts,
        out_specs=pl.BlockSpec((tt, RW + CW), lambda b, t: (b * n_t + t, 0)),
        out_shape=jax.ShapeDtypeStruct((batch * seq, RW + CW), BF16),
        scratch_shapes=[
            pltpu.VMEM((N_GROUPS, GROUP, GROUP), F32),
            pltpu.VMEM((1, RWKV_COLS), F32),
            pltpu.VMEM((CONV_HIST + tt, CW), F32),
        ] + [pltpu.VMEM((tt, RW), F32) for _ in range(7)],
        compiler_params=pltpu.CompilerParams(
            dimension_semantics=("arbitrary", "arbitrary"), vmem_limit_bytes=VMEM_LIMIT),
        name="mixer",
    )(p, *params)


def kernel(x, norm1_g, w_in, mu_shift, w0, w_up, a0, a_up, g_up, k_k, k_a, r_k, gn_g, gn_b,
           dw_w, dw_b, cln_g, cln_b, w_out, norm2_g, w_ff1, w_ff2, final_g):
    batch, seq, d = x.shape
    depth = w_in.shape[0]
    assert d == D_MODEL and seq % CHUNK == 0 and (batch * seq) % ROW_TILE == 0
    x2 = x.reshape(batch * seq, d)

    idx = jnp.arange(GROUP) // HEAD
    ones_bd = (idx[:, None] == idx[None, :]).astype(BF16)
    tri = jnp.arange(CHUNK)
    ltri = (tri[None, :] <= tri[:, None]).astype(BF16)
    row = lambda a: a.reshape(1, -1)

    for l in range(depth):
        wa = jnp.zeros((DECAY_LORA + ICLR_LORA, 2 * RW), F32)
        wa = wa.at[:DECAY_LORA, :RW].set(w_up[l]).at[DECAY_LORA:, RW:].set(a_up[l]).astype(BF16)
        params = [row(mu_shift[l]), row(w0[l]), row(a0[l]), wa, g_up[l].astype(BF16),
                  row(k_k[l]), row(k_a[l]), row(r_k[l]), row(gn_g[l]), row(gn_b[l]),
                  dw_w[l], row(dw_b[l]), row(cln_g[l]), row(cln_b[l]), ones_bd, ltri]
        p = _in_proj(x2, row(norm1_g[l]), w_in[l].astype(BF16))
        mixed = _mixer(p, batch, seq, params)
        x2 = _out_ffn(x2, mixed, w_out[l].astype(BF16), row(norm2_g[l]),
                      w_ff1[l].astype(BF16), w_ff2[l].astype(BF16), row(final_g),
                      final=(l == depth - 1))
    return x2.reshape(batch, seq, d)
```

```python
import functools

import jax
import jax.numpy as jnp
from jax import lax
from jax.experimental import pallas as pl
from jax.experimental.pallas import tpu as pltpu

F32 = jnp.float32
BF16 = jnp.bfloat16

D_MODEL = 1024
RW = 512
HEAD = 64
CW = 512
DECAY_LORA = 64
ICLR_LORA = 64
GATE_LORA = 128
CONV_K = 31
D_FF = 4 * D_MODEL
NORM_EPS = 1e-5
LN_EPS = 1e-5
GN_EPS = 64e-5
L2_EPS = 1e-12

OFF_K = RW
OFF_V = 2 * RW
OFF_WD = 3 * RW
OFF_GD = OFF_WD + DECAY_LORA + ICLR_LORA
RWKV_COLS = OFF_GD + GATE_LORA
IN_COLS = RWKV_COLS + 2 * CW

CHUNK = 64
GROUP = 256
N_GROUPS = RW // GROUP
HEADS_PER_GROUP = GROUP // HEAD
CONV_HIST = 32

ROW_TILE = 512
MIX_TILE = 256
FF_SPLIT = 4
VMEM_LIMIT = 56 * 1024 * 1024

NN = ((1,), (0,))
NT = ((1,), (1,))


def _dot(a, b, dims=NN):
    return lax.dot_general(a, b, (dims, ((), ())), preferred_element_type=F32)


def _split(x):
    hi = x.astype(BF16)
    lo = (x - hi.astype(F32)).astype(BF16)
    return hi, lo


def _rmsnorm(x, g):
    ms = jnp.mean(x * x, axis=-1, keepdims=True)
    return x * lax.rsqrt(ms + NORM_EPS) * g


def _inproj_kernel(x_ref, g_ref, w_ref, p_ref):
    h = _rmsnorm(x_ref[...], g_ref[...]).astype(BF16)
    p_ref[...] = _dot(h, w_ref[...])


def _const_spec(shape):
    return pl.BlockSpec(shape, lambda *_: (0,) * len(shape), pipeline_mode=pl.Buffered(1))


def _in_proj(x2, g, w_bf):
    m = x2.shape[0]
    return pl.pallas_call(
        _inproj_kernel,
        grid=(m // ROW_TILE,),
        in_specs=[
            pl.BlockSpec((ROW_TILE, D_MODEL), lambda i: (i, 0)),
            _const_spec((1, D_MODEL)),
            _const_spec((D_MODEL, IN_COLS)),
        ],
        out_specs=pl.BlockSpec((ROW_TILE, IN_COLS), lambda i: (i, 0)),
        out_shape=jax.ShapeDtypeStruct((m, IN_COLS), F32),
        compiler_params=pltpu.CompilerParams(
            dimension_semantics=("arbitrary",), vmem_limit_bytes=VMEM_LIMIT),
        name="in_proj",
    )(x2, g, w_bf)


def _outffn_kernel(x_ref, mix_ref, wout_ref, g2_ref, w1_ref, w2_ref, gf_ref, o_ref, *, final):
    x1 = x_ref[...] + _dot(mix_ref[...], wout_ref[...])
    h2 = _rmsnorm(x1, g2_ref[...]).astype(BF16)
    acc = x1
    blk = D_FF // FF_SPLIT
    for j in range(FF_SPLIT):
        hid = _dot(h2, w1_ref[:, j * blk:(j + 1) * blk])
        hid = jnp.square(jnp.maximum(hid, 0.0)).astype(BF16)
        acc = acc + _dot(hid, w2_ref[j * blk:(j + 1) * blk, :])
    if final:
        acc = _rmsnorm(acc, gf_ref[...])
    o_ref[...] = acc


def _out_ffn(x2, mixed, wout_bf, g2, w1_bf, w2_bf, gf, final):
    m = x2.shape[0]
    return pl.pallas_call(
        functools.partial(_outffn_kernel, final=final),
        grid=(m // ROW_TILE,),
        in_specs=[
            pl.BlockSpec((ROW_TILE, D_MODEL), lambda i: (i, 0)),
            pl.BlockSpec((ROW_TILE, D_MODEL), lambda i: (i, 0)),
            _const_spec((D_MODEL, D_MODEL)),
            _const_spec((1, D_MODEL)),
            _const_spec((D_MODEL, D_FF)),
            _const_spec((D_FF, D_MODEL)),
            _const_spec((1, D_MODEL)),
        ],
        out_specs=pl.BlockSpec((ROW_TILE, D_MODEL), lambda i: (i, 0)),
        out_shape=jax.ShapeDtypeStruct((m, D_MODEL), F32),
        compiler_params=pltpu.CompilerParams(
            dimension_semantics=("arbitrary",), vmem_limit_bytes=VMEM_LIMIT),
        name="out_ffn",
    )(x2, mixed, wout_bf, g2, w1_bf, w2_bf, gf)


def _head_sum(x, ones_bd):
    parts = []
    for g in range(N_GROUPS):
        hi, lo = _split(x[:, g * GROUP:(g + 1) * GROUP])
        parts.append(_dot(hi, ones_bd) + _dot(lo, ones_bd))
    return jnp.concatenate(parts, axis=-1)


def _block_diag(q, bd_mask):
    return jnp.concatenate([q] * HEADS_PER_GROUP, axis=0) * bd_mask


def _hmm(a, b, bd_mask):
    return _dot(a.astype(BF16), _block_diag(b.astype(BF16), bd_mask))


def _mixer_kernel(p_ref, mu_ref, w0_ref, a0_ref, wa_ref, gup_ref, kk_ref, ka_ref, rk_ref,
                  gng_ref, gnb_ref, dww_ref, dwb_ref, clg_ref, clb_ref, ones_ref, ltri_ref,
                  o_ref,
                  s_ref, carry_ref, conv_ref, r_s, k_s, v_s, kk_s, b_s, lw_s, y_s, u0_s, gate_s, bonus_s,
                  ap_s, rp_s, bh_s, kh_s, at_s, kt_s, bt_s, ge_s):
    tt = p_ref.shape[0]
    t_idx = pl.program_id(1)

    @pl.when(t_idx == 0)
    def _():
        s_ref[...] = jnp.zeros_like(s_ref)
        carry_ref[...] = jnp.zeros_like(carry_ref)
        conv_ref[0:CONV_HIST, :] = jnp.zeros((CONV_HIST, CW), F32)

    ones_bd = ones_ref[...]
    mu = mu_ref[...]
    row0 = lax.broadcasted_iota(jnp.int32, (tt, 1), 0) == 0

    def shifted(lo, hi):
        cur = p_ref[:, lo:hi]
        prev = pltpu.roll(cur, 1, axis=0)
        prev = jnp.where(row0, carry_ref[:, lo:hi], prev)
        return cur + (prev - cur) * mu[:, lo:hi]

    r = shifted(0, OFF_K)
    k = shifted(OFF_K, OFF_V)
    v = shifted(OFF_V, OFF_WD)
    wa = shifted(OFF_WD, OFF_GD)
    gd = shifted(OFF_GD, RWKV_COLS)
    carry_ref[...] = p_ref[tt - 1:tt, 0:RWKV_COLS]

    lane = lax.broadcasted_iota(jnp.int32, wa.shape, 1)
    wa = jnp.where(lane < DECAY_LORA, jnp.tanh(wa), wa)
    lora = _dot(wa.astype(BF16), wa_ref[...])
    z = w0_ref[...] + lora[:, :RW]
    w_log = -(jnp.maximum(-z, 0.0) + jnp.log1p(jnp.exp(-jnp.abs(z)))) - 0.5
    lw_s[...] = -jnp.exp(w_log)
    a = jax.nn.sigmoid(a0_ref[...] + lora[:, RW:])
    gate_s[...] = _dot(jax.nn.sigmoid(gd).astype(BF16), gup_ref[...])

    kk = k * kk_ref[...]
    inv_norm = jnp.minimum(lax.rsqrt(_head_sum(kk * kk, ones_bd)), 1.0 / L2_EPS)
    kk = kk * inv_norm
    k = k * (1.0 + (a - 1.0) * ka_ref[...])
    bonus_s[...] = _head_sum(r * k * rk_ref[...], ones_bd) * v
    r_s[...] = r
    k_s[...] = k
    v_s[...] = v
    kk_s[...] = kk
    b_s[...] = kk * a

    bd_mask = ones_bd
    ltri = ltri_ref[...]
    row = lax.broadcasted_iota(jnp.int32, (CHUNK, GROUP), 0)
    col = lax.broadcasted_iota(jnp.int32, (CHUNK, GROUP), 1) & (HEAD - 1)
    strict = col < row
    incl = col <= row
    diff = row ^ col
    n_chunks = tt // CHUNK
    items = [(slice(c * CHUNK, (c + 1) * CHUNK), slice(g * GROUP, (g + 1) * GROUP))
             for c in range(n_chunks) for g in range(N_GROUPS)]

    for c in range(n_chunks):
        rows = slice(c * CHUNK, (c + 1) * CHUNK)
        lw = lw_s[rows, :]
        lw_hi, lw_lo = _split(lw)
        cum = _dot(ltri, lw_hi) + _dot(ltri, lw_lo)
        cum_end = cum[CHUNK - 1:CHUNK, :]
        e_neg = jnp.exp(-cum)
        e_rem = jnp.exp(cum_end - cum)
        k_c = k_s[rows, :]
        b_c = b_s[rows, :]
        r_s[rows, :] = r_s[rows, :] * jnp.exp(cum)
        at_s[rows, :] = (-kk_s[rows, :] * jnp.exp(cum - lw)).astype(BF16)
        kt_s[rows, :] = (k_c * e_neg).astype(BF16)
        bt_s[rows, :] = (b_c * e_neg).astype(BF16)
        kh_s[rows, :] = (k_c * e_rem).astype(BF16)
        bh_s[rows, :] = (b_c * e_rem).astype(BF16)
        ge_s[c * 8:(c + 1) * 8, :] = jnp.broadcast_to(jnp.exp(cum_end), (8, RW))

    n_ls, a_aks, a_rbs, a_rks = [], [], [], []
    for rows, sl in items:
        lhs = jnp.concatenate([at_s[rows, sl], r_s[rows, sl].astype(BF16)], axis=0)
        ab = _dot(lhs, _block_diag(bt_s[rows, sl], bd_mask), NT)
        ak = _dot(lhs, _block_diag(kt_s[rows, sl], bd_mask), NT)
        n_ls.append(jnp.where(strict, ab[:CHUNK], 0.0))
        a_aks.append(jnp.where(strict, ak[:CHUNK], 0.0).astype(BF16))
        a_rbs.append(jnp.where(incl, ab[CHUNK:], 0.0).astype(BF16))
        a_rks.append(jnp.where(incl, ak[CHUNK:], 0.0).astype(BF16))

    xs = [jnp.where(diff == 0, 1.0, 0.0) + jnp.where(diff == 1, n_l, 0.0) for n_l in n_ls]
    for lvl in range(1, 6):
        ys = [_hmm(x, jnp.where((diff >> lvl) == 1, n_l, 0.0), bd_mask) for x, n_l in zip(xs, n_ls)]
        xs = [x + _hmm(y, x, bd_mask) for x, y in zip(xs, ys)]
    t_invs = [x.astype(BF16) for x in xs]

    v_bds = [_block_diag(v_s[rows, sl].astype(BF16), bd_mask) for rows, sl in items]
    a_ps = [_dot(t, _block_diag(at_s[rows, sl], bd_mask)).astype(BF16)
            for t, (rows, sl) in zip(t_invs, items)]
    w0s = [_dot(a_ak, v_bd).astype(BF16) for a_ak, v_bd in zip(a_aks, v_bds)]
    u0s = [_dot(t, _block_diag(w0, bd_mask)) for t, w0 in zip(t_invs, w0s)]
    for i, (rows, sl) in enumerate(items):
        ap_s[rows, sl] = a_ps[i]
        rp_s[rows, sl] = (r_s[rows, sl] + _dot(a_rbs[i], _block_diag(a_ps[i], bd_mask))).astype(BF16)
        u0_s[rows, sl] = u0s[i]
        y_s[rows, sl] = (_dot(a_rbs[i], _block_diag(u0s[i].astype(BF16), bd_mask))
                         + _dot(a_rks[i], v_bds[i]))

    rid = lax.broadcasted_iota(jnp.int32, (GROUP, GROUP), 0) // HEAD
    cid = lax.broadcasted_iota(jnp.int32, (GROUP, GROUP), 1) // HEAD
    same_head = rid == cid
    states = [s_ref[g] for g in range(N_GROUPS)]
    for c in range(n_chunks):
        rows = slice(c * CHUNK, (c + 1) * CHUNK)
        for g in range(N_GROUPS):
            sl = slice(g * GROUP, (g + 1) * GROUP)
            s = states[g]
            s_hi, s_lo = _split(s)
            lhs = jnp.concatenate([ap_s[rows, sl], rp_s[rows, sl]], axis=0)
            us = _dot(lhs, s_hi, NT) + _dot(lhs, s_lo, NT)
            u = us[:CHUNK] + u0_s[rows, sl]
            y_s[rows, sl] = y_s[rows, sl] + us[CHUNK:]
            uv_t = jnp.concatenate([u, v_s[rows, sl]], axis=0).T.astype(BF16)
            bk = jnp.concatenate([bh_s[rows, sl], kh_s[rows, sl]], axis=0)
            states[g] = jnp.where(same_head, s * ge_s[c * 8:c * 8 + 1, sl] + _dot(uv_t, bk), 0.0)
    for g in range(N_GROUPS):
        s_ref[g] = states[g]

    y = y_s[...]
    mean = _head_sum(y, ones_bd) * (1.0 / HEAD)
    yc = y - mean
    var = _head_sum(yc * yc, ones_bd) * (1.0 / HEAD)
    yn = yc * lax.rsqrt(var + GN_EPS) * gng_ref[...] + gnb_ref[...]
    o_ref[:, 0:RW] = ((yn + bonus_s[...]) * gate_s[...]).astype(o_ref.dtype)

    u = p_ref[:, RWKV_COLS:RWKV_COLS + CW] * jax.nn.sigmoid(p_ref[:, RWKV_COLS + CW:IN_COLS])
    conv_ref[CONV_HIST:CONV_HIST + tt, :] = u
    acc = jnp.zeros((tt, CW), F32) + dwb_ref[...]
    base = CONV_HIST - (CONV_K - 1)
    for j in range(CONV_K):
        acc = acc + conv_ref[base + j:base + j + tt, :] * dww_ref[j:j + 1, :]
    conv_ref[0:CONV_HIST, :] = conv_ref[tt:tt + CONV_HIST, :]
    mu_c = jnp.mean(acc, axis=-1, keepdims=True)
    xc = acc - mu_c
    var_c = jnp.mean(xc * xc, axis=-1, keepdims=True)
    ln = xc * lax.rsqrt(var_c + LN_EPS) * clg_ref[...] + clb_ref[...]
    o_ref[:, RW:RW + CW] = (ln * jax.nn.sigmoid(ln)).astype(o_ref.dtype)


def _mixer(p, batch, seq, params):
    tt = min(MIX_TILE, seq)
    n_t = seq // tt
    consts = [_const_spec(a.shape) for a in params]
    return pl.pallas_call(
        _mixer_kernel,
        grid=(batch, n_t),
        in_specs=[pl.BlockSpec((tt, IN_COLS), lambda b, t: (b * n_t + t, 0))] + consts,
        out_specs=pl.BlockSpec((tt, RW + CW), lambda b, t: (b * n_t + t, 0)),
        out_shape=jax.ShapeDtypeStruct((batch * seq, RW + CW), BF16),
        scratch_shapes=[
            pltpu.VMEM((N_GROUPS, GROUP, GROUP), F32),
            pltpu.VMEM((1, RWKV_COLS), F32),
            pltpu.VMEM((CONV_HIST + tt, CW), F32),
        ] + [pltpu.VMEM((tt, RW), F32) for _ in range(10)]
        + [pltpu.VMEM((tt, RW), BF16) for _ in range(7)]
        + [pltpu.VMEM((tt // CHUNK * 8, RW), F32)],
        compiler_params=pltpu.CompilerParams(
            dimension_semantics=("arbitrary", "arbitrary"), vmem_limit_bytes=VMEM_LIMIT),
        name="mixer",
    )(p, *params)


def kernel(x, norm1_g, w_in, mu_shift, w0, w_up, a0, a_up, g_up, k_k, k_a, r_k, gn_g, gn_b,
           dw_w, dw_b, cln_g, cln_b, w_out, norm2_g, w_ff1, w_ff2, final_g):
    batch, seq, d = x.shape
    depth = w_in.shape[0]
    assert d == D_MODEL and seq % CHUNK == 0 and (batch * seq) % ROW_TILE == 0
    x2 = x.reshape(batch * seq, d)

    idx = jnp.arange(GROUP) // HEAD
    ones_bd = (idx[:, None] == idx[None, :]).astype(BF16)
    tri = jnp.arange(CHUNK)
    ltri = (tri[None, :] <= tri[:, None]).astype(BF16)
    row = lambda a: a.reshape(1, -1)

    for l in range(depth):
        wa = jnp.zeros((DECAY_LORA + ICLR_LORA, 2 * RW), F32)
        wa = wa.at[:DECAY_LORA, :RW].set(w_up[l]).at[DECAY_LORA:, RW:].set(a_up[l]).astype(BF16)
        params = [row(mu_shift[l]), row(w0[l]), row(a0[l]), wa, g_up[l].astype(BF16),
                  row(k_k[l]), row(k_a[l]), row(r_k[l]), row(gn_g[l]), row(gn_b[l]),
                  dw_w[l], row(dw_b[l]), row(cln_g[l]), row(cln_b[l]), ones_bd, ltri]
        p = _in_proj(x2, row(norm1_g[l]), w_in[l].astype(BF16))
        mixed = _mixer(p, batch, seq, params)
        x2 = _out_ffn(x2, mixed, w_out[l].astype(BF16), row(norm2_g[l]),
                      w_ff1[l].astype(BF16), w_ff2[l].astype(BF16), row(final_g),
                      final=(l == depth - 1))
    return x2.reshape(batch, seq, d)
```

```python
import functools

import jax
import jax.numpy as jnp
from jax import lax
from jax.experimental import pallas as pl
from jax.experimental.pallas import tpu as pltpu

F32 = jnp.float32
BF16 = jnp.bfloat16

D_MODEL = 1024
RW = 512
HEAD = 64
CW = 512
DECAY_LORA = 64
ICLR_LORA = 64
GATE_LORA = 128
CONV_K = 31
D_FF = 4 * D_MODEL
NORM_EPS = 1e-5
LN_EPS = 1e-5
GN_EPS = 64e-5
L2_EPS = 1e-12

OFF_K = RW
OFF_V = 2 * RW
OFF_WD = 3 * RW
OFF_GD = OFF_WD + DECAY_LORA + ICLR_LORA
RWKV_COLS = OFF_GD + GATE_LORA
IN_COLS = RWKV_COLS + 2 * CW

CHUNK = 64
GROUP = 256
N_GROUPS = RW // GROUP
HEADS_PER_GROUP = GROUP // HEAD
SUBLANES = 8
CONV_HIST = 32

ROW_TILE = 512
MIX_TILE = 256
FF_SPLIT = 4
VMEM_LIMIT = 56 * 1024 * 1024

NN = ((1,), (0,))
NT = ((1,), (1,))


def _dot(a, b, dims=NN):
    return lax.dot_general(a, b, (dims, ((), ())), preferred_element_type=F32)


def _split(x):
    hi = x.astype(BF16)
    lo = (x - hi.astype(F32)).astype(BF16)
    return hi, lo


def _rmsnorm(x, g):
    ms = jnp.mean(x * x, axis=-1, keepdims=True)
    return x * lax.rsqrt(ms + NORM_EPS) * g


def _const_spec(shape):
    return pl.BlockSpec(shape, lambda *_: (0,) * len(shape), pipeline_mode=pl.Buffered(1))


def _outffn_kernel(x_ref, mix_ref, wout_ref, g2_ref, w1_ref, w2_ref, gf_ref, o_ref, *, final):
    x1 = x_ref[...] + _dot(mix_ref[...], wout_ref[...])
    h2 = _rmsnorm(x1, g2_ref[...]).astype(BF16)
    acc = x1
    blk = D_FF // FF_SPLIT
    for j in range(FF_SPLIT):
        hid = _dot(h2, w1_ref[:, j * blk:(j + 1) * blk])
        hid = jnp.square(jnp.maximum(hid, 0.0)).astype(BF16)
        acc = acc + _dot(hid, w2_ref[j * blk:(j + 1) * blk, :])
    if final:
        acc = _rmsnorm(acc, gf_ref[...])
    o_ref[...] = acc


def _out_ffn(x2, mixed, wout_bf, g2, w1_bf, w2_bf, gf, final):
    m = x2.shape[0]
    return pl.pallas_call(
        functools.partial(_outffn_kernel, final=final),
        grid=(m // ROW_TILE,),
        in_specs=[
            pl.BlockSpec((ROW_TILE, D_MODEL), lambda i: (i, 0)),
            pl.BlockSpec((ROW_TILE, D_MODEL), lambda i: (i, 0)),
            _const_spec((D_MODEL, D_MODEL)),
            _const_spec((1, D_MODEL)),
            _const_spec((D_MODEL, D_FF)),
            _const_spec((D_FF, D_MODEL)),
            _const_spec((1, D_MODEL)),
        ],
        out_specs=pl.BlockSpec((ROW_TILE, D_MODEL), lambda i: (i, 0)),
        out_shape=jax.ShapeDtypeStruct((m, D_MODEL), F32),
        compiler_params=pltpu.CompilerParams(
            dimension_semantics=("arbitrary",), vmem_limit_bytes=VMEM_LIMIT),
        name="out_ffn",
    )(x2, mixed, wout_bf, g2, w1_bf, w2_bf, gf)


def _head_sum(x, ones_bd):
    parts = []
    for g in range(N_GROUPS):
        hi, lo = _split(x[:, g * GROUP:(g + 1) * GROUP])
        parts.append(_dot(hi, ones_bd) + _dot(lo, ones_bd))
    return jnp.concatenate(parts, axis=-1)


def _block_diag(q, bd_mask):
    return jnp.concatenate([q] * HEADS_PER_GROUP, axis=0) * bd_mask


def _hmm(a, b, bd_mask):
    return _dot(a.astype(BF16), _block_diag(b.astype(BF16), bd_mask))


def _mixer_kernel(x_ref, g1_ref, win_ref, mu_ref, w0_ref, a0_ref, wa_ref, gup_ref, kk_ref, ka_ref, rk_ref,
                  gng_ref, gnb_ref, dww_ref, dwb_ref, clg_ref, clb_ref, ones_ref, ltri_ref,
                  o_ref,
                  s_ref, carry_ref, conv_ref, r_s, k_s, v_s, kk_s, b_s, lw_s, y_s, u0_s, gate_s, bonus_s,
                  ap_s, rp_s, bh_s, kh_s, at_s, kt_s, bt_s, ge_s, shift_ref):
    tt = x_ref.shape[0]
    t_idx = pl.program_id(1)

    @pl.when(t_idx == 0)
    def _():
        s_ref[...] = jnp.zeros_like(s_ref)
        carry_ref[...] = jnp.zeros_like(carry_ref)
        conv_ref[0:CONV_HIST, :] = jnp.zeros((CONV_HIST, CW), F32)

    ones_bd = ones_ref[...]
    mu = mu_ref[...]
    row0 = lax.broadcasted_iota(jnp.int32, (tt, 1), 0) == 0

    h = _rmsnorm(x_ref[...], g1_ref[...]).astype(BF16)

    def project(lo, hi):
        return _dot(h, win_ref[:, lo:hi])

    def shifted(lo, hi):
        cur = project(lo, hi)
        prev = pltpu.roll(cur, 1, axis=0)
        prev = jnp.where(row0, carry_ref[:, lo:hi], prev)
        carry_ref[:, lo:hi] = cur[tt - 1:tt, :]
        return cur + (prev - cur) * mu[:, lo:hi]

    r = shifted(0, OFF_K)
    k = shifted(OFF_K, OFF_V)
    v = shifted(OFF_V, OFF_WD)
    wa = shifted(OFF_WD, OFF_GD)
    gd = shifted(OFF_GD, RWKV_COLS)

    lane = lax.broadcasted_iota(jnp.int32, wa.shape, 1)
    wa = jnp.where(lane < DECAY_LORA, jnp.tanh(wa), wa)
    lora = _dot(wa.astype(BF16), wa_ref[...])
    z = w0_ref[...] + lora[:, :RW]
    w_log = -(jnp.maximum(-z, 0.0) + jnp.log1p(jnp.exp(-jnp.abs(z)))) - 0.5
    lw_s[...] = -jnp.exp(w_log)
    a = jax.nn.sigmoid(a0_ref[...] + lora[:, RW:])
    gate_s[...] = _dot(jax.nn.sigmoid(gd).astype(BF16), gup_ref[...])

    kk = k * kk_ref[...]
    inv_norm = jnp.minimum(lax.rsqrt(_head_sum(kk * kk, ones_bd)), 1.0 / L2_EPS)
    kk = kk * inv_norm
    k = k * (1.0 + (a - 1.0) * ka_ref[...])
    bonus_s[...] = _head_sum(r * k * rk_ref[...], ones_bd) * v
    r_s[...] = r
    k_s[...] = k
    v_s[...] = v
    kk_s[...] = kk
    b_s[...] = kk * a

    bd_mask = ones_bd
    ltri = ltri_ref[...]
    row = lax.broadcasted_iota(jnp.int32, (CHUNK, GROUP), 0)
    col = lax.broadcasted_iota(jnp.int32, (CHUNK, GROUP), 1) & (HEAD - 1)
    strict = col < row
    incl = col <= row
    diff = row ^ col
    n_chunks = tt // CHUNK
    items = [(slice(c * CHUNK, (c + 1) * CHUNK), slice(g * GROUP, (g + 1) * GROUP))
             for c in range(n_chunks) for g in range(N_GROUPS)]

    for c in range(n_chunks):
        rows = slice(c * CHUNK, (c + 1) * CHUNK)
        lw = lw_s[rows, :]
        lw_hi, lw_lo = _split(lw)
        cum = _dot(ltri, lw_hi) + _dot(ltri, lw_lo)
        cum_end = cum[CHUNK - 1:CHUNK, :]
        e_neg = jnp.exp(-cum)
        e_rem = jnp.exp(cum_end - cum)
        k_c = k_s[rows, :]
        b_c = b_s[rows, :]
        r_s[rows, :] = r_s[rows, :] * jnp.exp(cum)
        at_s[rows, :] = (-kk_s[rows, :] * jnp.exp(cum - lw)).astype(BF16)
        kt_s[rows, :] = (k_c * e_neg).astype(BF16)
        bt_s[rows, :] = (b_c * e_neg).astype(BF16)
        kh_s[rows, :] = (k_c * e_rem).astype(BF16)
        bh_s[rows, :] = (b_c * e_rem).astype(BF16)
        ge_s[c * 8:(c + 1) * 8, :] = jnp.broadcast_to(jnp.exp(cum_end), (8, RW))

    conv_ref[CONV_HIST:CONV_HIST + tt, :] = (
        project(RWKV_COLS, RWKV_COLS + CW) * jax.nn.sigmoid(project(RWKV_COLS + CW, IN_COLS)))
    span = tt + CONV_HIST - SUBLANES
    for ph in range(1, SUBLANES):
        shift_ref[ph - 1] = conv_ref[ph:ph + span, :]

    n_ls, a_rbs = [], []
    for rows, sl in items:
        lhs = jnp.concatenate([at_s[rows, sl], r_s[rows, sl].astype(BF16)], axis=0)
        ab = _dot(lhs, _block_diag(bt_s[rows, sl], bd_mask), NT)
        n_ls.append(jnp.where(strict, ab[:CHUNK], 0.0))
        a_rbs.append(jnp.where(incl, ab[CHUNK:], 0.0).astype(BF16))

    xs = [jnp.where(diff == 0, 1.0, 0.0) + jnp.where(diff == 1, n_l, 0.0) for n_l in n_ls]
    for lvl in range(1, 6):
        ys = [_hmm(x, jnp.where((diff >> lvl) == 1, n_l, 0.0), bd_mask) for x, n_l in zip(xs, n_ls)]
        xs = [x + _hmm(y, x, bd_mask) for x, y in zip(xs, ys)]
    t_invs = [x.astype(BF16) for x in xs]

    a_ps, w0s = [], []
    for t_inv, (rows, sl) in zip(t_invs, items):
        lhs = jnp.concatenate([at_s[rows, sl], r_s[rows, sl].astype(BF16)], axis=0)
        ak = _dot(lhs, _block_diag(kt_s[rows, sl], bd_mask), NT)
        v_bd = _block_diag(v_s[rows, sl].astype(BF16), bd_mask)
        w0s.append(_dot(jnp.where(strict, ak[:CHUNK], 0.0).astype(BF16), v_bd).astype(BF16))
        y_s[rows, sl] = _dot(jnp.where(incl, ak[CHUNK:], 0.0).astype(BF16), v_bd)
        a_ps.append(_dot(t_inv, _block_diag(at_s[rows, sl], bd_mask)).astype(BF16))
    u0s = []
    for i, (rows, sl) in enumerate(items):
        u0s.append(_dot(t_invs[i], _block_diag(w0s[i], bd_mask)))
        ap_s[rows, sl] = a_ps[i]
        rp_s[rows, sl] = (r_s[rows, sl] + _dot(a_rbs[i], _block_diag(a_ps[i], bd_mask))).astype(BF16)
    for i, (rows, sl) in enumerate(items):
        u0_s[rows, sl] = u0s[i]
        y_s[rows, sl] = y_s[rows, sl] + _dot(a_rbs[i], _block_diag(u0s[i].astype(BF16), bd_mask))

    rid = lax.broadcasted_iota(jnp.int32, (GROUP, GROUP), 0) // HEAD
    cid = lax.broadcasted_iota(jnp.int32, (GROUP, GROUP), 1) // HEAD
    same_head = rid == cid
    groups = [slice(g * GROUP, (g + 1) * GROUP) for g in range(N_GROUPS)]

    def state_read(rows):
        us_parts = []
        for g, sl in enumerate(groups):
            s = s_ref[g]
            lhs = jnp.concatenate([ap_s[rows, sl], rp_s[rows, sl]], axis=0)
            us = _dot(lhs, s.astype(BF16), NT)
            y_s[rows, sl] = y_s[rows, sl] + us[CHUNK:]
            us_parts.append((s, us[:CHUNK] + u0_s[rows, sl]))
        return us_parts

    def state_write(rows, ge_row, us_parts):
        for g, sl in enumerate(groups):
            s, u = us_parts[g]
            uv_t = jnp.concatenate([u, v_s[rows, sl]], axis=0).T.astype(BF16)
            bk = jnp.concatenate([bh_s[rows, sl], kh_s[rows, sl]], axis=0)
            s_ref[g] = jnp.where(same_head, s * ge_s[ge_row, sl] + _dot(uv_t, bk), 0.0)

    def conv_rows(row0, rows):
        acc = jnp.zeros((CHUNK, CW), F32) + dwb_ref[...]
        for j in range(CONV_K):
            off = CONV_HIST - (CONV_K - 1) + j
            ph = off % SUBLANES
            src = conv_ref if ph == 0 else shift_ref.at[ph - 1]
            acc = acc + src[pl.ds(row0 + (off - ph), CHUNK), :] * dww_ref[j:j + 1, :]
        mu_c = jnp.mean(acc, axis=-1, keepdims=True)
        xc = acc - mu_c
        var_c = jnp.mean(xc * xc, axis=-1, keepdims=True)
        ln = xc * lax.rsqrt(var_c + LN_EPS) * clg_ref[...] + clb_ref[...]
        o_ref[rows, RW:RW + CW] = (ln * jax.nn.sigmoid(ln)).astype(o_ref.dtype)

    def gn_center(rows):
        y = y_s[rows, :]
        return y - _head_sum(y, ones_bd) * (1.0 / HEAD)

    def gn_finish(rows, yc):
        var = _head_sum(yc * yc, ones_bd) * (1.0 / HEAD)
        yn = yc * lax.rsqrt(var + GN_EPS) * gng_ref[...] + gnb_ref[...]
        o_ref[rows, 0:RW] = ((yn + bonus_s[rows, :]) * gate_s[rows, :]).astype(o_ref.dtype)

    first = slice(0, CHUNK)
    state_write(first, slice(0, 1), state_read(first))
    conv_rows(0, first)

    def chunk_step(c, carry):
        row0 = pl.multiple_of(c * CHUNK, CHUNK)
        rows = pl.ds(row0, CHUNK)
        prev = pl.ds(pl.multiple_of(row0 - CHUNK, CHUNK), CHUNK)
        us_parts = state_read(rows)
        yc = gn_center(prev)
        state_write(rows, pl.ds(pl.multiple_of(c * SUBLANES, SUBLANES), 1), us_parts)
        gn_finish(prev, yc)
        conv_rows(row0, rows)
        return carry

    lax.fori_loop(1, n_chunks, chunk_step, 0)
    last = slice(tt - CHUNK, tt)
    gn_finish(last, gn_center(last))
    conv_ref[0:CONV_HIST, :] = conv_ref[tt:tt + CONV_HIST, :]


def _mixer(x2, batch, seq, params):
    tt = min(MIX_TILE, seq)
    n_t = seq // tt
    consts = [_const_spec(a.shape) for a in params]
    return pl.pallas_call(
        _mixer_kernel,
        grid=(batch, n_t),
        in_specs=[pl.BlockSpec((tt, D_MODEL), lambda b, t: (b * n_t + t, 0))] + consts,
        out_specs=pl.BlockSpec((tt, RW + CW), lambda b, t: (b * n_t + t, 0)),
        out_shape=jax.ShapeDtypeStruct((batch * seq, RW + CW), BF16),
        scratch_shapes=[
            pltpu.VMEM((N_GROUPS, GROUP, GROUP), F32),
            pltpu.VMEM((1, RWKV_COLS), F32),
            pltpu.VMEM((CONV_HIST + tt, CW), F32),
        ] + [pltpu.VMEM((tt, RW), F32) for _ in range(10)]
        + [pltpu.VMEM((tt, RW), BF16) for _ in range(7)]
        + [pltpu.VMEM((tt // CHUNK * 8, RW), F32),
           pltpu.VMEM((SUBLANES - 1, CONV_HIST + tt - SUBLANES, CW), F32)],
        compiler_params=pltpu.CompilerParams(
            dimension_semantics=("arbitrary", "arbitrary"), vmem_limit_bytes=VMEM_LIMIT),
        name="mixer",
    )(x2, *params)


def kernel(x, norm1_g, w_in, mu_shift, w0, w_up, a0, a_up, g_up, k_k, k_a, r_k, gn_g, gn_b,
           dw_w, dw_b, cln_g, cln_b, w_out, norm2_g, w_ff1, w_ff2, final_g):
    batch, seq, d = x.shape
    depth = w_in.shape[0]
    assert d == D_MODEL and seq % CHUNK == 0 and (batch * seq) % ROW_TILE == 0
    x2 = x.reshape(batch * seq, d)

    idx = jnp.arange(GROUP) // HEAD
    ones_bd = (idx[:, None] == idx[None, :]).astype(BF16)
    tri = jnp.arange(CHUNK)
    ltri = (tri[None, :] <= tri[:, None]).astype(BF16)
    row = lambda a: a.reshape(1, -1)

    for l in range(depth):
        wa = jnp.zeros((DECAY_LORA + ICLR_LORA, 2 * RW), F32)
        wa = wa.at[:DECAY_LORA, :RW].set(w_up[l]).at[DECAY_LORA:, RW:].set(a_up[l]).astype(BF16)
        params = [row(norm1_g[l]), w_in[l].astype(BF16), row(mu_shift[l]), row(w0[l]), row(a0[l]), wa, g_up[l].astype(BF16),
                  row(k_k[l]), row(k_a[l]), row(r_k[l]), row(gn_g[l]), row(gn_b[l]),
                  dw_w[l], row(dw_b[l]), row(cln_g[l]), row(cln_b[l]), ones_bd, ltri]
        mixed = _mixer(x2, batch, seq, params)
        x2 = _out_ffn(x2, mixed, w_out[l].astype(BF16), row(norm2_g[l]),
                      w_ff1[l].astype(BF16), w_ff2[l].astype(BF16), row(final_g),
                      final=(l == depth - 1))
    return x2.reshape(batch, seq, d)
```

```python
import functools

import jax
import jax.numpy as jnp
from jax import lax
from jax.experimental import pallas as pl
from jax.experimental.pallas import tpu as pltpu

F32 = jnp.float32
BF16 = jnp.bfloat16

D_MODEL = 1024
RW = 512
HEAD = 64
CW = 512
DECAY_LORA = 64
ICLR_LORA = 64
GATE_LORA = 128
CONV_K = 31
D_FF = 4 * D_MODEL
NORM_EPS = 1e-5
LN_EPS = 1e-5
GN_EPS = 64e-5
L2_EPS = 1e-12
DECAY_SCALE = 0.6065306597126334

OFF_K = RW
OFF_V = 2 * RW
OFF_WD = 3 * RW
OFF_GD = OFF_WD + DECAY_LORA + ICLR_LORA
RWKV_COLS = OFF_GD + GATE_LORA
IN_COLS = RWKV_COLS + 2 * CW

CHUNK = 64
GROUP = 256
N_GROUPS = RW // GROUP
HEADS_PER_GROUP = GROUP // HEAD
SUBLANES = 8
CONV_HIST = 32

ROW_TILE = 512
MIX_TILE = 256
SEQS = 2
FF_SPLIT = 4
VMEM_LIMIT = 56 * 1024 * 1024

NN = ((1,), (0,))
NT = ((1,), (1,))


def _dot(a, b, dims=NN):
    return lax.dot_general(a, b, (dims, ((), ())), preferred_element_type=F32)


def _split(x):
    hi = x.astype(BF16)
    lo = (x - hi.astype(F32)).astype(BF16)
    return hi, lo


def _rmsnorm(x, g):
    ms = jnp.mean(x * x, axis=-1, keepdims=True)
    return x * lax.rsqrt(ms + NORM_EPS) * g


def _const_spec(shape):
    return pl.BlockSpec(shape, lambda *_: (0,) * len(shape), pipeline_mode=pl.Buffered(1))


def _outffn_kernel(x_ref, mix_ref, wout_ref, g2_ref, w1_ref, w2_ref, gf_ref, o_ref, *, final):
    x1 = x_ref[...] + _dot(mix_ref[...], wout_ref[...])
    h2 = _rmsnorm(x1, g2_ref[...]).astype(BF16)
    acc = x1
    blk = D_FF // FF_SPLIT
    for j in range(FF_SPLIT):
        hid = _dot(h2, w1_ref[:, j * blk:(j + 1) * blk])
        hid = jnp.square(jnp.maximum(hid, 0.0)).astype(BF16)
        acc = acc + _dot(hid, w2_ref[j * blk:(j + 1) * blk, :])
    if final:
        acc = _rmsnorm(acc, gf_ref[...])
    o_ref[...] = acc


def _out_ffn(x2, mixed, wout_bf, g2, w1_bf, w2_bf, gf, final):
    m = x2.shape[0]
    return pl.pallas_call(
        functools.partial(_outffn_kernel, final=final),
        grid=(m // ROW_TILE,),
        in_specs=[
            pl.BlockSpec((ROW_TILE, D_MODEL), lambda i: (i, 0)),
            pl.BlockSpec((ROW_TILE, D_MODEL), lambda i: (i, 0)),
            _const_spec((D_MODEL, D_MODEL)),
            _const_spec((1, D_MODEL)),
            _const_spec((D_MODEL, D_FF)),
            _const_spec((D_FF, D_MODEL)),
            _const_spec((1, D_MODEL)),
        ],
        out_specs=pl.BlockSpec((ROW_TILE, D_MODEL), lambda i: (i, 0)),
        out_shape=jax.ShapeDtypeStruct((m, D_MODEL), F32),
        compiler_params=pltpu.CompilerParams(
            dimension_semantics=("arbitrary",), vmem_limit_bytes=VMEM_LIMIT),
        name="out_ffn",
    )(x2, mixed, wout_bf, g2, w1_bf, w2_bf, gf)


def _head_sum(x, ones_bd):
    parts = [_dot(x[:, g * GROUP:(g + 1) * GROUP].astype(BF16), ones_bd) for g in range(N_GROUPS)]
    return jnp.concatenate(parts, axis=-1)


def _block_diag(q, bd_mask):
    return jnp.concatenate([q] * HEADS_PER_GROUP, axis=0) * bd_mask


def _hmm(a, b, bd_mask):
    return _dot(a.astype(BF16), _block_diag(b.astype(BF16), bd_mask))


def _mixer_kernel(x_ref, g1_ref, win_ref, mu_ref, w0_ref, a0_ref, wa_ref, gup_ref, kk_ref, ka_ref, rk_ref,
                  gng_ref, gnb_ref, dww_ref, dwb_ref, clg_ref, clb_ref, ones_ref, ltri_ref,
                  o_ref,
                  s_ref, carry_ref, conv_ref, r_s, k_s, v_s, kk_s, b_s, lw_s, y_s, u0_s, gate_s, bonus_s,
                  ap_s, rp_s, bh_s, kh_s, at_s, kt_s, bt_s, ge_s, shift_ref):
    tt = x_ref.shape[1]
    n_rows = SEQS * tt
    t_idx = pl.program_id(1)

    @pl.when(t_idx == 0)
    def _():
        s_ref[...] = jnp.zeros_like(s_ref)
        carry_ref[...] = jnp.zeros_like(carry_ref)
        conv_ref[:, 0:CONV_HIST, :] = jnp.zeros((SEQS, CONV_HIST, CW), F32)

    ones_bd = ones_ref[...]
    mu = mu_ref[...]
    first_row = lax.broadcasted_iota(jnp.int32, (SUBLANES, 1), 0) == 0

    h = _rmsnorm(x_ref[...].reshape(n_rows, D_MODEL), g1_ref[...]).astype(BF16)

    def project(lo, hi):
        return _dot(h, win_ref[:, lo:hi])

    def shifted(lo, hi):
        cur = project(lo, hi)
        prev = pltpu.roll(cur, 1, axis=0)
        pieces = []
        for j in range(SEQS):
            head = jnp.where(first_row, carry_ref[j:j + 1, lo:hi], prev[j * tt:j * tt + SUBLANES, :])
            pieces += [head, prev[j * tt + SUBLANES:(j + 1) * tt, :]]
            carry_ref[j:j + 1, lo:hi] = cur[(j + 1) * tt - 1:(j + 1) * tt, :]
        prev = jnp.concatenate(pieces, axis=0)
        return cur + (prev - cur) * mu[:, lo:hi]

    r = shifted(0, OFF_K)
    k = shifted(OFF_K, OFF_V)
    v = shifted(OFF_V, OFF_WD)
    wa = shifted(OFF_WD, OFF_GD)
    gd = shifted(OFF_GD, RWKV_COLS)

    lane = lax.broadcasted_iota(jnp.int32, wa.shape, 1)
    wa = jnp.where(lane < DECAY_LORA, jnp.tanh(wa), wa)
    lora = _dot(wa.astype(BF16), wa_ref[...])
    z = w0_ref[...] + lora[:, :RW]
    lw_s[...] = jax.nn.sigmoid(z) * (-DECAY_SCALE)
    a = jax.nn.sigmoid(a0_ref[...] + lora[:, RW:])
    gate_s[...] = _dot(jax.nn.sigmoid(gd).astype(BF16), gup_ref[...])

    kk = k * kk_ref[...]
    inv_norm = jnp.minimum(lax.rsqrt(_head_sum(kk * kk, ones_bd)), 1.0 / L2_EPS)
    kk = kk * inv_norm
    k = k * (1.0 + (a - 1.0) * ka_ref[...])
    bonus_s[...] = _head_sum(r * k * rk_ref[...], ones_bd) * v
    r_s[...] = r
    k_s[...] = k
    v_s[...] = v
    kk_s[...] = kk
    b_s[...] = kk * a

    bd_mask = ones_bd
    ltri = ltri_ref[...]
    row = lax.broadcasted_iota(jnp.int32, (CHUNK, GROUP), 0)
    col = lax.broadcasted_iota(jnp.int32, (CHUNK, GROUP), 1) & (HEAD - 1)
    strict = col < row
    incl = col <= row
    diff = row ^ col
    chunk_row = lax.broadcasted_iota(jnp.int32, (CHUNK, 1), 0)
    n_chunks = tt // CHUNK
    all_chunks = SEQS * n_chunks
    items = [(slice(c * CHUNK, (c + 1) * CHUNK), slice(g * GROUP, (g + 1) * GROUP))
             for c in range(all_chunks) for g in range(N_GROUPS)]

    for c in range(all_chunks):
        rows = slice(c * CHUNK, (c + 1) * CHUNK)
        lw = lw_s[rows, :]
        lw_hi, lw_lo = _split(lw)
        cum = _dot(ltri, lw_hi) + _dot(ltri, lw_lo)
        e_pos = jnp.exp(cum)
        e_neg = 1.0 / e_pos
        g_end = e_pos[CHUNK - 1:CHUNK, :]
        e_rem = g_end * e_neg
        k_c = k_s[rows, :]
        b_c = b_s[rows, :]
        r_s[rows, :] = r_s[rows, :] * e_pos
        e_prev = jnp.where(chunk_row == 0, 1.0, pltpu.roll(e_pos, 1, axis=0))
        at_s[rows, :] = (-kk_s[rows, :] * e_prev).astype(BF16)
        kt_s[rows, :] = (k_c * e_neg).astype(BF16)
        bt_s[rows, :] = (b_c * e_neg).astype(BF16)
        kh_s[rows, :] = (k_c * e_rem).astype(BF16)
        bh_s[rows, :] = (b_c * e_rem).astype(BF16)
        ge_s[c * SUBLANES:(c + 1) * SUBLANES, :] = jnp.broadcast_to(g_end, (SUBLANES, RW))

    glu = project(RWKV_COLS, RWKV_COLS + CW) * jax.nn.sigmoid(project(RWKV_COLS + CW, IN_COLS))
    span = tt + CONV_HIST - SUBLANES
    for j in range(SEQS):
        conv_ref[j, CONV_HIST:CONV_HIST + tt, :] = glu[j * tt:(j + 1) * tt, :]
        for ph in range(1, SUBLANES):
            shift_ref[j, ph - 1] = conv_ref[j, ph:ph + span, :]

    n_ls, a_rbs = [], []
    for rows, sl in items:
        lhs = jnp.concatenate([at_s[rows, sl], r_s[rows, sl].astype(BF16)], axis=0)
        ab = _dot(lhs, _block_diag(bt_s[rows, sl], bd_mask), NT)
        n_ls.append(jnp.where(strict, ab[:CHUNK], 0.0))
        a_rbs.append(jnp.where(incl, ab[CHUNK:], 0.0).astype(BF16))

    xs = [jnp.where(diff == 0, 1.0, 0.0) + jnp.where(diff == 1, n_l, 0.0) for n_l in n_ls]
    for lvl in range(1, 6):
        ys = [_hmm(x, jnp.where((diff >> lvl) == 1, n_l, 0.0), bd_mask) for x, n_l in zip(xs, n_ls)]
        xs = [x + _hmm(y, x, bd_mask) for x, y in zip(xs, ys)]
    t_invs = [x.astype(BF16) for x in xs]

    a_ps, w0s = [], []
    for t_inv, (rows, sl) in zip(t_invs, items):
        lhs = jnp.concatenate([at_s[rows, sl], r_s[rows, sl].astype(BF16)], axis=0)
        ak = _dot(lhs, _block_diag(kt_s[rows, sl], bd_mask), NT)
        v_bd = _block_diag(v_s[rows, sl].astype(BF16), bd_mask)
        w0s.append(_dot(jnp.where(strict, ak[:CHUNK], 0.0).astype(BF16), v_bd).astype(BF16))
        y_s[rows, sl] = _dot(jnp.where(incl, ak[CHUNK:], 0.0).astype(BF16), v_bd)
        a_ps.append(_dot(t_inv, _block_diag(at_s[rows, sl], bd_mask)).astype(BF16))
    u0s = []
    for i, (rows, sl) in enumerate(items):
        u0s.append(_dot(t_invs[i], _block_diag(w0s[i], bd_mask)))
        ap_s[rows, sl] = a_ps[i]
        rp_s[rows, sl] = (r_s[rows, sl] + _dot(a_rbs[i], _block_diag(a_ps[i], bd_mask))).astype(BF16)
    for i, (rows, sl) in enumerate(items):
        u0_s[rows, sl] = u0s[i]
        y_s[rows, sl] = y_s[rows, sl] + _dot(a_rbs[i], _block_diag(u0s[i].astype(BF16), bd_mask))

    rid = lax.broadcasted_iota(jnp.int32, (GROUP, GROUP), 0) // HEAD
    cid = lax.broadcasted_iota(jnp.int32, (GROUP, GROUP), 1) // HEAD
    same_head = rid == cid
    groups = [slice(g * GROUP, (g + 1) * GROUP) for g in range(N_GROUPS)]

    def state_read(j, rows):
        us_parts = []
        for g, sl in enumerate(groups):
            s = s_ref[j, g]
            lhs = jnp.concatenate([ap_s[rows, sl], rp_s[rows, sl]], axis=0)
            us = _dot(lhs, s.astype(BF16), NT)
            y_s[rows, sl] = y_s[rows, sl] + us[CHUNK:]
            us_parts.append((s, us[:CHUNK] + u0_s[rows, sl]))
        return us_parts

    def state_write(j, rows, ge_row, us_parts):
        for g, sl in enumerate(groups):
            s, u = us_parts[g]
            uv_t = jnp.concatenate([u, v_s[rows, sl]], axis=0).T.astype(BF16)
            bk = jnp.concatenate([bh_s[rows, sl], kh_s[rows, sl]], axis=0)
            s_ref[j, g] = jnp.where(same_head, s * ge_s[ge_row, sl] + _dot(uv_t, bk), 0.0)

    def conv_rows(j, t0, out_rows):
        acc = jnp.zeros((CHUNK, CW), F32) + dwb_ref[...]
        for tap in range(CONV_K):
            off = CONV_HIST - (CONV_K - 1) + tap
            ph = off % SUBLANES
            src = conv_ref.at[j] if ph == 0 else shift_ref.at[j, ph - 1]
            acc = acc + src[pl.ds(t0 + (off - ph), CHUNK), :] * dww_ref[tap:tap + 1, :]
        mu_c = jnp.mean(acc, axis=-1, keepdims=True)
        xc = acc - mu_c
        var_c = jnp.mean(xc * xc, axis=-1, keepdims=True)
        ln = xc * lax.rsqrt(var_c + LN_EPS) * clg_ref[...] + clb_ref[...]
        o_ref[j, out_rows, RW:RW + CW] = (ln * jax.nn.sigmoid(ln)).astype(o_ref.dtype)

    def gn_center(rows):
        y = y_s[rows, :]
        return y - _head_sum(y, ones_bd) * (1.0 / HEAD)

    def gn_finish(j, rows, out_rows, yc):
        var = _head_sum(yc * yc, ones_bd) * (1.0 / HEAD)
        yn = yc * lax.rsqrt(var + GN_EPS) * gng_ref[...] + gnb_ref[...]
        o_ref[j, out_rows, 0:RW] = ((yn + bonus_s[rows, :]) * gate_s[rows, :]).astype(o_ref.dtype)

    seqs = range(SEQS)
    first = [slice(j * tt, j * tt + CHUNK) for j in seqs]
    parts = [state_read(j, first[j]) for j in seqs]
    for j in seqs:
        state_write(j, first[j], slice(j * n_chunks * SUBLANES, j * n_chunks * SUBLANES + 1), parts[j])
    for j in seqs:
        conv_rows(j, 0, slice(0, CHUNK))

    def chunk_step(c, carry):
        t0 = pl.multiple_of(c * CHUNK, CHUNK)
        out_rows = pl.ds(t0, CHUNK)
        out_prev = pl.ds(pl.multiple_of(t0 - CHUNK, CHUNK), CHUNK)
        rows = [pl.ds(pl.multiple_of(j * tt + t0, CHUNK), CHUNK) for j in seqs]
        prev = [pl.ds(pl.multiple_of(j * tt + t0 - CHUNK, CHUNK), CHUNK) for j in seqs]
        ge_rows = [pl.ds(pl.multiple_of((j * n_chunks + c) * SUBLANES, SUBLANES), 1) for j in seqs]
        parts = [state_read(j, rows[j]) for j in seqs]
        ycs = [gn_center(prev[j]) for j in seqs]
        for j in seqs:
            state_write(j, rows[j], ge_rows[j], parts[j])
        for j in seqs:
            gn_finish(j, prev[j], out_prev, ycs[j])
        for j in seqs:
            conv_rows(j, t0, out_rows)
        return carry

    lax.fori_loop(1, n_chunks, chunk_step, 0)
    last_out = slice(tt - CHUNK, tt)
    for j in seqs:
        last = slice((j + 1) * tt - CHUNK, (j + 1) * tt)
        gn_finish(j, last, last_out, gn_center(last))
        conv_ref[j, 0:CONV_HIST, :] = conv_ref[j, tt:tt + CONV_HIST, :]


def _mixer(x3, params):
    batch, seq, _ = x3.shape
    tt = min(MIX_TILE, seq)
    rows = SEQS * tt
    consts = [_const_spec(a.shape) for a in params]
    return pl.pallas_call(
        _mixer_kernel,
        grid=(batch // SEQS, seq // tt),
        in_specs=[pl.BlockSpec((SEQS, tt, D_MODEL), lambda b, t: (b, t, 0))] + consts,
        out_specs=pl.BlockSpec((SEQS, tt, RW + CW), lambda b, t: (b, t, 0)),
        out_shape=jax.ShapeDtypeStruct((batch, seq, RW + CW), BF16),
        scratch_shapes=[
            pltpu.VMEM((SEQS, N_GROUPS, GROUP, GROUP), F32),
            pltpu.VMEM((SEQS, RWKV_COLS), F32),
            pltpu.VMEM((SEQS, CONV_HIST + tt, CW), F32),
        ] + [pltpu.VMEM((rows, RW), F32) for _ in range(10)]
        + [pltpu.VMEM((rows, RW), BF16) for _ in range(7)]
        + [pltpu.VMEM((rows // CHUNK * SUBLANES, RW), F32),
           pltpu.VMEM((SEQS, SUBLANES - 1, CONV_HIST + tt - SUBLANES, CW), F32)],
        compiler_params=pltpu.CompilerParams(
            dimension_semantics=("arbitrary", "arbitrary"), vmem_limit_bytes=VMEM_LIMIT),
        name="mixer",
    )(x3, *params)


def kernel(x, norm1_g, w_in, mu_shift, w0, w_up, a0, a_up, g_up, k_k, k_a, r_k, gn_g, gn_b,
           dw_w, dw_b, cln_g, cln_b, w_out, norm2_g, w_ff1, w_ff2, final_g):
    batch, seq, d = x.shape
    depth = w_in.shape[0]
    assert d == D_MODEL and seq % CHUNK == 0 and (batch * seq) % ROW_TILE == 0 and batch % SEQS == 0
    x2 = x.reshape(batch * seq, d)

    idx = jnp.arange(GROUP) // HEAD
    ones_bd = (idx[:, None] == idx[None, :]).astype(BF16)
    tri = jnp.arange(CHUNK)
    ltri = (tri[None, :] <= tri[:, None]).astype(BF16)
    row = lambda a: a.reshape(1, -1)

    for l in range(depth):
        wa = jnp.zeros((DECAY_LORA + ICLR_LORA, 2 * RW), F32)
        wa = wa.at[:DECAY_LORA, :RW].set(w_up[l]).at[DECAY_LORA:, RW:].set(a_up[l]).astype(BF16)
        params = [row(norm1_g[l]), w_in[l].astype(BF16), row(mu_shift[l]), row(w0[l]), row(a0[l]), wa,
                  g_up[l].astype(BF16), row(k_k[l]), row(k_a[l]), row(r_k[l]), row(gn_g[l]), row(gn_b[l]),
                  dw_w[l], row(dw_b[l]), row(cln_g[l]), row(cln_b[l]), ones_bd, ltri]
        mixed = _mixer(x2.reshape(batch, seq, d), params).reshape(batch * seq, RW + CW)
        x2 = _out_ffn(x2, mixed, w_out[l].astype(BF16), row(norm2_g[l]),
                      w_ff1[l].astype(BF16), w_ff2[l].astype(BF16), row(final_g),
                      final=(l == depth - 1))
    return x2.reshape(batch, seq, d)
```

```python
import functools

import jax
import jax.numpy as jnp
from jax import lax
from jax.experimental import pallas as pl
from jax.experimental.pallas import tpu as pltpu

F32 = jnp.float32
BF16 = jnp.bfloat16

D_MODEL = 1024
RW = 512
HEAD = 64
CW = 512
DECAY_LORA = 64
ICLR_LORA = 64
GATE_LORA = 128
CONV_K = 31
D_FF = 4 * D_MODEL
NORM_EPS = 1e-5
LN_EPS = 1e-5
GN_EPS = 64e-5
L2_EPS = 1e-12
DECAY_SCALE = 0.6065306597126334

OFF_K = RW
OFF_V = 2 * RW
OFF_WD = 3 * RW
OFF_GD = OFF_WD + DECAY_LORA + ICLR_LORA
RWKV_COLS = OFF_GD + GATE_LORA
IN_COLS = RWKV_COLS + 2 * CW

CHUNK = 64
GROUP = 256
N_GROUPS = RW // GROUP
HEADS_PER_GROUP = GROUP // HEAD
SUBLANES = 8
CONV_HIST = 32

ROW_TILE = 512
MIX_TILE = 256
SEQS = 2
FF_SPLIT = 4
VMEM_LIMIT = 56 * 1024 * 1024

NN = ((1,), (0,))
NT = ((1,), (1,))


def _dot(a, b, dims=NN):
    return lax.dot_general(a, b, (dims, ((), ())), preferred_element_type=F32)


def _split(x):
    hi = x.astype(BF16)
    lo = (x - hi.astype(F32)).astype(BF16)
    return hi, lo


def _rmsnorm(x, g):
    ms = jnp.mean(x * x, axis=-1, keepdims=True)
    return x * lax.rsqrt(ms + NORM_EPS) * g


def _const_spec(shape):
    return pl.BlockSpec(shape, lambda *_: (0,) * len(shape), pipeline_mode=pl.Buffered(1))


def _outffn_kernel(x_ref, mix_ref, wout_ref, g2_ref, w1_ref, w2_ref, gf_ref, o_ref, *, final):
    x1 = x_ref[...] + _dot(mix_ref[...], wout_ref[...])
    h2 = _rmsnorm(x1, g2_ref[...]).astype(BF16)
    acc = x1
    blk = D_FF // FF_SPLIT
    for j in range(FF_SPLIT):
        hid = _dot(h2, w1_ref[:, j * blk:(j + 1) * blk])
        hid = jnp.square(jnp.maximum(hid, 0.0)).astype(BF16)
        acc = acc + _dot(hid, w2_ref[j * blk:(j + 1) * blk, :])
    if final:
        acc = _rmsnorm(acc, gf_ref[...])
    o_ref[...] = acc


def _out_ffn(x2, mixed, wout_bf, g2, w1_bf, w2_bf, gf, final):
    m = x2.shape[0]
    return pl.pallas_call(
        functools.partial(_outffn_kernel, final=final),
        grid=(m // ROW_TILE,),
        in_specs=[
            pl.BlockSpec((ROW_TILE, D_MODEL), lambda i: (i, 0)),
            pl.BlockSpec((ROW_TILE, D_MODEL), lambda i: (i, 0)),
            _const_spec((D_MODEL, D_MODEL)),
            _const_spec((1, D_MODEL)),
            _const_spec((D_MODEL, D_FF)),
            _const_spec((D_FF, D_MODEL)),
            _const_spec((1, D_MODEL)),
        ],
        out_specs=pl.BlockSpec((ROW_TILE, D_MODEL), lambda i: (i, 0)),
        out_shape=jax.ShapeDtypeStruct((m, D_MODEL), F32),
        compiler_params=pltpu.CompilerParams(
            dimension_semantics=("arbitrary",), vmem_limit_bytes=VMEM_LIMIT),
        name="out_ffn",
    )(x2, mixed, wout_bf, g2, w1_bf, w2_bf, gf)


def _head_sum(x, ones_bd):
    parts = [_dot(x[:, g * GROUP:(g + 1) * GROUP].astype(BF16), ones_bd) for g in range(N_GROUPS)]
    return jnp.concatenate(parts, axis=-1)


def _block_diag(q, bd_mask):
    return jnp.concatenate([q] * HEADS_PER_GROUP, axis=0) * bd_mask


def _hmm(a, b, bd_mask):
    return _dot(a.astype(BF16), _block_diag(b.astype(BF16), bd_mask))


def _mixer_kernel(x_ref, g1_ref, win_ref, mu_ref, w0_ref, a0_ref, wa_ref, gup_ref, kk_ref, ka_ref, rk_ref,
                  gng_ref, gnb_ref, dww_ref, dwb_ref, clg_ref, clb_ref, ones_ref, ltri_ref,
                  o_ref,
                  s_ref, carry_ref, conv_ref, r_s, k_s, v_s, kk_s, b_s, lw_s, y_s, u0_s, gate_s, bonus_s,
                  ap_s, rp_s, bh_s, kh_s, at_s, kt_s, bt_s, ge_s, shift_ref):
    tt = x_ref.shape[1]
    n_rows = SEQS * tt
    t_idx = pl.program_id(1)

    @pl.when(t_idx == 0)
    def _():
        s_ref[...] = jnp.zeros_like(s_ref)
        carry_ref[...] = jnp.zeros_like(carry_ref)
        conv_ref[:, 0:CONV_HIST, :] = jnp.zeros((SEQS, CONV_HIST, CW), F32)

    ones_bd = ones_ref[...]
    mu = mu_ref[...]
    first_row = lax.broadcasted_iota(jnp.int32, (SUBLANES, 1), 0) == 0

    h = _rmsnorm(x_ref[...].reshape(n_rows, D_MODEL), g1_ref[...]).astype(BF16)

    def project(lo, hi):
        return _dot(h, win_ref[:, lo:hi])

    def shifted(lo, hi):
        cur = project(lo, hi)
        prev = pltpu.roll(cur, 1, axis=0)
        pieces = []
        for j in range(SEQS):
            head = jnp.where(first_row, carry_ref[j:j + 1, lo:hi], prev[j * tt:j * tt + SUBLANES, :])
            pieces += [head, prev[j * tt + SUBLANES:(j + 1) * tt, :]]
            carry_ref[j:j + 1, lo:hi] = cur[(j + 1) * tt - 1:(j + 1) * tt, :]
        prev = jnp.concatenate(pieces, axis=0)
        return cur + (prev - cur) * mu[:, lo:hi]

    r = shifted(0, OFF_K)
    k = shifted(OFF_K, OFF_V)
    v = shifted(OFF_V, OFF_WD)
    wa = shifted(OFF_WD, OFF_GD)
    gd = shifted(OFF_GD, RWKV_COLS)

    glu = project(RWKV_COLS, RWKV_COLS + CW) * jax.nn.sigmoid(project(RWKV_COLS + CW, IN_COLS))
    span = tt + CONV_HIST - SUBLANES
    for j in range(SEQS):
        conv_ref[j, CONV_HIST:CONV_HIST + tt, :] = glu[j * tt:(j + 1) * tt, :]
        for ph in range(1, SUBLANES):
            shift_ref[j, ph - 1] = conv_ref[j, ph:ph + span, :]

    lane = lax.broadcasted_iota(jnp.int32, wa.shape, 1)
    wa = jnp.where(lane < DECAY_LORA, jnp.tanh(wa), wa)
    lora = _dot(wa.astype(BF16), wa_ref[...])
    z = w0_ref[...] + lora[:, :RW]
    lw_s[...] = jax.nn.sigmoid(z) * (-DECAY_SCALE)
    a = jax.nn.sigmoid(a0_ref[...] + lora[:, RW:])
    gate_s[...] = _dot(jax.nn.sigmoid(gd).astype(BF16), gup_ref[...])

    kk = k * kk_ref[...]
    inv_norm = jnp.minimum(lax.rsqrt(_head_sum(kk * kk, ones_bd)), 1.0 / L2_EPS)
    kk = kk * inv_norm
    k = k * (1.0 + (a - 1.0) * ka_ref[...])
    bonus_s[...] = _head_sum(r * k * rk_ref[...], ones_bd) * v
    r_s[...] = r
    k_s[...] = k
    v_s[...] = v
    kk_s[...] = kk
    b_s[...] = kk * a

    bd_mask = ones_bd
    ltri = ltri_ref[...]
    row = lax.broadcasted_iota(jnp.int32, (CHUNK, GROUP), 0)
    col = lax.broadcasted_iota(jnp.int32, (CHUNK, GROUP), 1) & (HEAD - 1)
    strict = col < row
    incl = col <= row
    diff = row ^ col
    chunk_row = lax.broadcasted_iota(jnp.int32, (CHUNK, 1), 0)
    n_chunks = tt // CHUNK
    all_chunks = SEQS * n_chunks
    items = [(slice(c * CHUNK, (c + 1) * CHUNK), slice(g * GROUP, (g + 1) * GROUP))
             for c in range(all_chunks) for g in range(N_GROUPS)]

    for c in range(all_chunks):
        rows = slice(c * CHUNK, (c + 1) * CHUNK)
        lw = lw_s[rows, :]
        lw_hi, lw_lo = _split(lw)
        cum = _dot(ltri, lw_hi) + _dot(ltri, lw_lo)
        e_pos = jnp.exp(cum)
        e_neg = 1.0 / e_pos
        g_end = e_pos[CHUNK - 1:CHUNK, :]
        e_rem = g_end * e_neg
        k_c = k_s[rows, :]
        b_c = b_s[rows, :]
        r_s[rows, :] = r_s[rows, :] * e_pos
        e_prev = jnp.where(chunk_row == 0, 1.0, pltpu.roll(e_pos, 1, axis=0))
        at_s[rows, :] = (-kk_s[rows, :] * e_prev).astype(BF16)
        kt_s[rows, :] = (k_c * e_neg).astype(BF16)
        bt_s[rows, :] = (b_c * e_neg).astype(BF16)
        kh_s[rows, :] = (k_c * e_rem).astype(BF16)
        bh_s[rows, :] = (b_c * e_rem).astype(BF16)
        ge_s[c * SUBLANES:(c + 1) * SUBLANES, :] = jnp.broadcast_to(g_end, (SUBLANES, RW))

    n_ls, a_rbs = [], []
    for rows, sl in items:
        lhs = jnp.concatenate([at_s[rows, sl], r_s[rows, sl].astype(BF16)], axis=0)
        ab = _dot(lhs, _block_diag(bt_s[rows, sl], bd_mask), NT)
        n_ls.append(jnp.where(strict, ab[:CHUNK], 0.0))
        a_rbs.append(jnp.where(incl, ab[CHUNK:], 0.0).astype(BF16))

    xs = [jnp.where(diff == 0, 1.0, 0.0) + jnp.where(diff == 1, n_l, 0.0) for n_l in n_ls]
    for lvl in range(1, 6):
        ys = [_hmm(x, jnp.where((diff >> lvl) == 1, n_l, 0.0), bd_mask) for x, n_l in zip(xs, n_ls)]
        xs = [x + _hmm(y, x, bd_mask) for x, y in zip(xs, ys)]
    t_invs = [x.astype(BF16) for x in xs]

    aks = []
    for rows, sl in items:
        lhs = jnp.concatenate([at_s[rows, sl], r_s[rows, sl].astype(BF16)], axis=0)
        ak = _dot(lhs, _block_diag(kt_s[rows, sl], bd_mask), NT)
        aks.append((jnp.where(strict, ak[:CHUNK], 0.0).astype(BF16),
                    jnp.where(incl, ak[CHUNK:], 0.0).astype(BF16)))
    a_ps = [_dot(t_inv, _block_diag(at_s[rows, sl], bd_mask)).astype(BF16)
            for t_inv, (rows, sl) in zip(t_invs, items)]
    w0s = []
    for (a_ak, a_rk), (rows, sl) in zip(aks, items):
        v_bd = _block_diag(v_s[rows, sl].astype(BF16), bd_mask)
        w0s.append(_dot(a_ak, v_bd).astype(BF16))
        y_s[rows, sl] = _dot(a_rk, v_bd)
    u0s = []
    for i, (rows, sl) in enumerate(items):
        u0s.append(_dot(t_invs[i], _block_diag(w0s[i], bd_mask)))
        ap_s[rows, sl] = a_ps[i]
        rp_s[rows, sl] = (r_s[rows, sl] + _dot(a_rbs[i], _block_diag(a_ps[i], bd_mask))).astype(BF16)
    for i, (rows, sl) in enumerate(items):
        u0_s[rows, sl] = u0s[i]
        y_s[rows, sl] = y_s[rows, sl] + _dot(a_rbs[i], _block_diag(u0s[i].astype(BF16), bd_mask))

    rid = lax.broadcasted_iota(jnp.int32, (GROUP, GROUP), 0) // HEAD
    cid = lax.broadcasted_iota(jnp.int32, (GROUP, GROUP), 1) // HEAD
    same_head = rid == cid
    groups = [slice(g * GROUP, (g + 1) * GROUP) for g in range(N_GROUPS)]

    def state_read(j, rows):
        us_parts = []
        for g, sl in enumerate(groups):
            s = s_ref[j, g]
            lhs = jnp.concatenate([ap_s[rows, sl], rp_s[rows, sl]], axis=0)
            us = _dot(lhs, s.astype(BF16), NT)
            y_s[rows, sl] = y_s[rows, sl] + us[CHUNK:]
            us_parts.append((s, us[:CHUNK] + u0_s[rows, sl]))
        return us_parts

    def state_write(j, rows, ge_row, us_parts):
        for g, sl in enumerate(groups):
            s, u = us_parts[g]
            uv_t = jnp.concatenate([u, v_s[rows, sl]], axis=0).T.astype(BF16)
            bk = jnp.concatenate([bh_s[rows, sl], kh_s[rows, sl]], axis=0)
            s_ref[j, g] = jnp.where(same_head, s * ge_s[ge_row, sl] + _dot(uv_t, bk), 0.0)

    def conv_rows(j, t0, out_rows):
        acc = jnp.zeros((CHUNK, CW), F32) + dwb_ref[...]
        for tap in range(CONV_K):
            off = CONV_HIST - (CONV_K - 1) + tap
            ph = off % SUBLANES
            src = conv_ref.at[j] if ph == 0 else shift_ref.at[j, ph - 1]
            acc = acc + src[pl.ds(t0 + (off - ph), CHUNK), :] * dww_ref[tap:tap + 1, :]
        mu_c = jnp.mean(acc, axis=-1, keepdims=True)
        xc = acc - mu_c
        var_c = jnp.mean(xc * xc, axis=-1, keepdims=True)
        ln = xc * lax.rsqrt(var_c + LN_EPS) * clg_ref[...] + clb_ref[...]
        o_ref[j, out_rows, RW:RW + CW] = (ln * jax.nn.sigmoid(ln)).astype(o_ref.dtype)

    def gn_center(rows):
        y = y_s[rows, :]
        return y - _head_sum(y, ones_bd) * (1.0 / HEAD)

    def gn_finish(j, rows, out_rows, yc):
        var = _head_sum(yc * yc, ones_bd) * (1.0 / HEAD)
        yn = yc * lax.rsqrt(var + GN_EPS) * gng_ref[...] + gnb_ref[...]
        o_ref[j, out_rows, 0:RW] = ((yn + bonus_s[rows, :]) * gate_s[rows, :]).astype(o_ref.dtype)

    seqs = range(SEQS)
    first = [slice(j * tt, j * tt + CHUNK) for j in seqs]
    parts = [state_read(j, first[j]) for j in seqs]
    for j in seqs:
        state_write(j, first[j], slice(j * n_chunks * SUBLANES, j * n_chunks * SUBLANES + 1), parts[j])
    for j in seqs:
        conv_rows(j, 0, slice(0, CHUNK))

    def chunk_step(c, carry):
        t0 = pl.multiple_of(c * CHUNK, CHUNK)
        out_rows = pl.ds(t0, CHUNK)
        out_prev = pl.ds(pl.multiple_of(t0 - CHUNK, CHUNK), CHUNK)
        rows = [pl.ds(pl.multiple_of(j * tt + t0, CHUNK), CHUNK) for j in seqs]
        prev = [pl.ds(pl.multiple_of(j * tt + t0 - CHUNK, CHUNK), CHUNK) for j in seqs]
        ge_rows = [pl.ds(pl.multiple_of((j * n_chunks + c) * SUBLANES, SUBLANES), 1) for j in seqs]
        for j in seqs:
            conv_rows(j, t0, out_rows)
        parts = [state_read(j, rows[j]) for j in seqs]
        ycs = [gn_center(prev[j]) for j in seqs]
        for j in seqs:
            state_write(j, rows[j], ge_rows[j], parts[j])
        for j in seqs:
            gn_finish(j, prev[j], out_prev, ycs[j])
        return carry

    lax.fori_loop(1, n_chunks, chunk_step, 0)
    last_out = slice(tt - CHUNK, tt)
    for j in seqs:
        last = slice((j + 1) * tt - CHUNK, (j + 1) * tt)
        gn_finish(j, last, last_out, gn_center(last))
        conv_ref[j, 0:CONV_HIST, :] = conv_ref[j, tt:tt + CONV_HIST, :]


def _mixer(x3, params):
    batch, seq, _ = x3.shape
    tt = min(MIX_TILE, seq)
    rows = SEQS * tt
    consts = [_const_spec(a.shape) for a in params]
    return pl.pallas_call(
        _mixer_kernel,
        grid=(batch // SEQS, seq // tt),
        in_specs=[pl.BlockSpec((SEQS, tt, D_MODEL), lambda b, t: (b, t, 0))] + consts,
        out_specs=pl.BlockSpec((SEQS, tt, RW + CW), lambda b, t: (b, t, 0)),
        out_shape=jax.ShapeDtypeStruct((batch, seq, RW + CW), BF16),
        scratch_shapes=[
            pltpu.VMEM((SEQS, N_GROUPS, GROUP, GROUP), F32),
            pltpu.VMEM((SEQS, RWKV_COLS), F32),
            pltpu.VMEM((SEQS, CONV_HIST + tt, CW), F32),
        ] + [pltpu.VMEM((rows, RW), F32) for _ in range(10)]
        + [pltpu.VMEM((rows, RW), BF16) for _ in range(7)]
        + [pltpu.VMEM((rows // CHUNK * SUBLANES, RW), F32),
           pltpu.VMEM((SEQS, SUBLANES - 1, CONV_HIST + tt - SUBLANES, CW), F32)],
        compiler_params=pltpu.CompilerParams(
            dimension_semantics=("arbitrary", "arbitrary"), vmem_limit_bytes=VMEM_LIMIT),
        name="mixer",
    )(x3, *params)


def kernel(x, norm1_g, w_in, mu_shift, w0, w_up, a0, a_up, g_up, k_k, k_a, r_k, gn_g, gn_b,
           dw_w, dw_b, cln_g, cln_b, w_out, norm2_g, w_ff1, w_ff2, final_g):
    batch, seq, d = x.shape
    depth = w_in.shape[0]
    assert d == D_MODEL and seq % CHUNK == 0 and (batch * seq) % ROW_TILE == 0 and batch % SEQS == 0
    x2 = x.reshape(batch * seq, d)

    idx = jnp.arange(GROUP) // HEAD
    ones_bd = (idx[:, None] == idx[None, :]).astype(BF16)
    tri = jnp.arange(CHUNK)
    ltri = (tri[None, :] <= tri[:, None]).astype(BF16)
    row = lambda a: a.reshape(1, -1)

    for l in range(depth):
        wa = jnp.zeros((DECAY_LORA + ICLR_LORA, 2 * RW), F32)
        wa = wa.at[:DECAY_LORA, :RW].set(w_up[l]).at[DECAY_LORA:, RW:].set(a_up[l]).astype(BF16)
        params = [row(norm1_g[l]), w_in[l].astype(BF16), row(mu_shift[l]), row(w0[l]), row(a0[l]), wa,
                  g_up[l].astype(BF16), row(k_k[l]), row(k_a[l]), row(r_k[l]), row(gn_g[l]), row(gn_b[l]),
                  dw_w[l], row(dw_b[l]), row(cln_g[l]), row(cln_b[l]), ones_bd, ltri]
        mixed = _mixer(x2.reshape(batch, seq, d), params).reshape(batch * seq, RW + CW)
        x2 = _out_ffn(x2, mixed, w_out[l].astype(BF16), row(norm2_g[l]),
                      w_ff1[l].astype(BF16), w_ff2[l].astype(BF16), row(final_g),
                      final=(l == depth - 1))
    return x2.reshape(batch, seq, d)
```

```python
import functools

import jax
import jax.numpy as jnp
from jax import lax
from jax.experimental import pallas as pl
from jax.experimental.pallas import tpu as pltpu

F32 = jnp.float32
BF16 = jnp.bfloat16

D_MODEL = 1024
RW = 512
HEAD = 64
CW = 512
DECAY_LORA = 64
ICLR_LORA = 64
GATE_LORA = 128
CONV_K = 31
D_FF = 4 * D_MODEL
NORM_EPS = 1e-5
LN_EPS = 1e-5
GN_EPS = 64e-5
L2_EPS = 1e-12
DECAY_SCALE = 0.6065306597126334

OFF_K = RW
OFF_V = 2 * RW
OFF_WD = 3 * RW
OFF_GD = OFF_WD + DECAY_LORA + ICLR_LORA
RWKV_COLS = OFF_GD + GATE_LORA
IN_COLS = RWKV_COLS + 2 * CW

CHUNK = 64
GROUP = 256
N_GROUPS = RW // GROUP
HEADS_PER_GROUP = GROUP // HEAD
SUBLANES = 8
CONV_HIST = 32

ROW_TILE = 512
MIX_TILE = 256
SEQS = 2
FF_SPLIT = 4
VMEM_LIMIT = 56 * 1024 * 1024

NN = ((1,), (0,))
NT = ((1,), (1,))


def _dot(a, b, dims=NN):
    return lax.dot_general(a, b, (dims, ((), ())), preferred_element_type=F32)


def _split(x):
    hi = x.astype(BF16)
    lo = (x - hi.astype(F32)).astype(BF16)
    return hi, lo


def _rmsnorm(x, g):
    ms = jnp.mean(x * x, axis=-1, keepdims=True)
    return x * lax.rsqrt(ms + NORM_EPS) * g


def _const_spec(shape):
    return pl.BlockSpec(shape, lambda *_: (0,) * len(shape), pipeline_mode=pl.Buffered(1))


def _outffn_kernel(x_ref, mix_ref, wout_ref, g2_ref, w1_ref, w2_ref, gf_ref, o_ref, *, final):
    x1 = x_ref[...] + _dot(mix_ref[...], wout_ref[...])
    h2 = _rmsnorm(x1, g2_ref[...]).astype(BF16)
    acc = x1
    blk = D_FF // FF_SPLIT
    for j in range(FF_SPLIT):
        hid = _dot(h2, w1_ref[:, j * blk:(j + 1) * blk])
        hid = jnp.square(jnp.maximum(hid, 0.0)).astype(BF16)
        acc = acc + _dot(hid, w2_ref[j * blk:(j + 1) * blk, :])
    if final:
        acc = _rmsnorm(acc, gf_ref[...])
    o_ref[...] = acc


def _out_ffn(x2, mixed, wout_bf, g2, w1_bf, w2_bf, gf, final):
    m = x2.shape[0]
    return pl.pallas_call(
        functools.partial(_outffn_kernel, final=final),
        grid=(m // ROW_TILE,),
        in_specs=[
            pl.BlockSpec((ROW_TILE, D_MODEL), lambda i: (i, 0)),
            pl.BlockSpec((ROW_TILE, D_MODEL), lambda i: (i, 0)),
            _const_spec((D_MODEL, D_MODEL)),
            _const_spec((1, D_MODEL)),
            _const_spec((D_MODEL, D_FF)),
            _const_spec((D_FF, D_MODEL)),
            _const_spec((1, D_MODEL)),
        ],
        out_specs=pl.BlockSpec((ROW_TILE, D_MODEL), lambda i: (i, 0)),
        out_shape=jax.ShapeDtypeStruct((m, D_MODEL), F32),
        compiler_params=pltpu.CompilerParams(
            dimension_semantics=("arbitrary",), vmem_limit_bytes=VMEM_LIMIT),
        name="out_ffn",
    )(x2, mixed, wout_bf, g2, w1_bf, w2_bf, gf)


def _head_sum(x, ones_bd):
    parts = [_dot(x[:, g * GROUP:(g + 1) * GROUP].astype(BF16), ones_bd) for g in range(N_GROUPS)]
    return jnp.concatenate(parts, axis=-1)


def _block_diag(q, bd_mask):
    return jnp.concatenate([q] * HEADS_PER_GROUP, axis=0) * bd_mask


def _hmm(a, b, bd_mask):
    return _dot(a.astype(BF16), _block_diag(b.astype(BF16), bd_mask))


def _mixer_kernel(x_ref, g1_ref, win_ref, mu_ref, w0_ref, a0_ref, wa_ref, gup_ref, kk_ref, ka_ref, rk_ref,
                  gng_ref, gnb_ref, dww_ref, dwb_ref, clg_ref, clb_ref, ones_ref, ltri_ref, zero_ref, lvl_ref,
                  o_ref,
                  s_ref, carry_ref, conv_ref, r_s, k_s, v_s, kk_s, b_s, lw_s, y_s, u0_s, gate_s, bonus_s,
                  ap_s, rp_s, bh_s, kh_s, at_s, kt_s, bt_s, ge_s, shift_ref):
    tt = x_ref.shape[1]
    n_rows = SEQS * tt
    t_idx = pl.program_id(1)

    @pl.when(t_idx == 0)
    def _():
        s_ref[...] = jnp.zeros_like(s_ref)
        carry_ref[...] = jnp.zeros_like(carry_ref)
        conv_ref[:, 0:CONV_HIST, :] = jnp.zeros((SEQS, CONV_HIST, CW), F32)

    ones_bd = ones_ref[...]
    mu = mu_ref[...]
    first_row = lax.broadcasted_iota(jnp.int32, (SUBLANES, 1), 0) == 0

    h = _rmsnorm(x_ref[...].reshape(n_rows, D_MODEL), g1_ref[...]).astype(BF16)

    def project(lo, hi):
        return _dot(h, win_ref[:, lo:hi])

    def shifted(lo, hi):
        cur = project(lo, hi)
        prev = pltpu.roll(cur, 1, axis=0)
        pieces = []
        for j in range(SEQS):
            head = jnp.where(first_row, carry_ref[j:j + 1, lo:hi], prev[j * tt:j * tt + SUBLANES, :])
            pieces += [head, prev[j * tt + SUBLANES:(j + 1) * tt, :]]
            carry_ref[j:j + 1, lo:hi] = cur[(j + 1) * tt - 1:(j + 1) * tt, :]
        prev = jnp.concatenate(pieces, axis=0)
        return cur + (prev - cur) * mu[:, lo:hi]

    r = shifted(0, OFF_K)
    k = shifted(OFF_K, OFF_V)
    v = shifted(OFF_V, OFF_WD)
    wa = shifted(OFF_WD, OFF_GD)
    gd = shifted(OFF_GD, RWKV_COLS)

    glu = project(RWKV_COLS, RWKV_COLS + CW) * jax.nn.sigmoid(project(RWKV_COLS + CW, IN_COLS))
    span = tt + CONV_HIST - SUBLANES
    for j in range(SEQS):
        conv_ref[j, CONV_HIST:CONV_HIST + tt, :] = glu[j * tt:(j + 1) * tt, :]
        for ph in range(1, SUBLANES):
            shift_ref[j, ph - 1] = conv_ref[j, ph:ph + span, :]

    lane = lax.broadcasted_iota(jnp.int32, wa.shape, 1)
    wa = jnp.where(lane < DECAY_LORA, jnp.tanh(wa), wa)
    lora = _dot(wa.astype(BF16), wa_ref[...])
    z = w0_ref[...] + lora[:, :RW]
    lw_s[...] = jax.nn.sigmoid(z) * (-DECAY_SCALE)
    a = jax.nn.sigmoid(a0_ref[...] + lora[:, RW:])
    gate_s[...] = _dot(jax.nn.sigmoid(gd).astype(BF16), gup_ref[...])

    kk = k * kk_ref[...]
    inv_norm = jnp.minimum(lax.rsqrt(_head_sum(kk * kk, ones_bd)), 1.0 / L2_EPS)
    kk = kk * inv_norm
    k = k * (1.0 + (a - 1.0) * ka_ref[...])
    bonus_s[...] = _head_sum(r * k * rk_ref[...], ones_bd) * v
    r_s[...] = r
    k_s[...] = k
    v_s[...] = v
    kk_s[...] = kk
    b_s[...] = kk * a

    bd_mask = ones_bd
    ltri = ltri_ref[...]
    row = lax.broadcasted_iota(jnp.int32, (CHUNK, GROUP), 0)
    col = lax.broadcasted_iota(jnp.int32, (CHUNK, GROUP), 1) & (HEAD - 1)
    strict = col < row
    incl = col <= row
    diff = row ^ col
    chunk_row = lax.broadcasted_iota(jnp.int32, (CHUNK, 1), 0)
    n_chunks = tt // CHUNK
    all_chunks = SEQS * n_chunks
    items = [(slice(c * CHUNK, (c + 1) * CHUNK), slice(g * GROUP, (g + 1) * GROUP))
             for c in range(all_chunks) for g in range(N_GROUPS)]

    for c in range(all_chunks):
        rows = slice(c * CHUNK, (c + 1) * CHUNK)
        lw = lw_s[rows, :]
        lw_hi, lw_lo = _split(lw)
        cum = _dot(ltri, lw_hi) + _dot(ltri, lw_lo)
        e_pos = jnp.exp(cum)
        e_neg = 1.0 / e_pos
        g_end = e_pos[CHUNK - 1:CHUNK, :]
        e_rem = g_end * e_neg
        k_c = k_s[rows, :]
        b_c = b_s[rows, :]
        r_s[rows, :] = r_s[rows, :] * e_pos
        e_prev = jnp.where(chunk_row == 0, 1.0, pltpu.roll(e_pos, 1, axis=0))
        at_s[rows, :] = (-kk_s[rows, :] * e_prev).astype(BF16)
        kt_s[rows, :] = (k_c * e_neg).astype(BF16)
        bt_s[rows, :] = (b_c * e_neg).astype(BF16)
        kh_s[rows, :] = (k_c * e_rem).astype(BF16)
        bh_s[rows, :] = (b_c * e_rem).astype(BF16)
        ge_s[c * SUBLANES:(c + 1) * SUBLANES, :] = jnp.broadcast_to(g_end, (SUBLANES, RW))

    n_ls, xs, a_rbs = [], [], []
    for rows, sl in items:
        lhs = jnp.concatenate([at_s[rows, sl], r_s[rows, sl].astype(BF16)], axis=0)
        ab = _dot(lhs, _block_diag(bt_s[rows, sl], bd_mask), NT)
        n_l = jnp.where(strict, ab[:CHUNK], 0.0)
        n_ls.append(n_l.astype(BF16))
        xs.append(jnp.where(diff == 0, 1.0, 0.0) + jnp.where(diff == 1, n_l, 0.0))
        a_rbs.append(jnp.where(incl, ab[CHUNK:], 0.0).astype(BF16))

    def conv_piece(j, c):
        acc = jnp.zeros((CHUNK, CW), F32) + dwb_ref[...]
        for tap in range(CONV_K):
            off = CONV_HIST - (CONV_K - 1) + tap + c * CHUNK
            ph = off % SUBLANES
            src = conv_ref.at[j] if ph == 0 else shift_ref.at[j, ph - 1]
            acc = acc + src[off - ph:off - ph + CHUNK, :] * dww_ref[tap:tap + 1, :]
        mu_c = jnp.mean(acc, axis=-1, keepdims=True)
        xc = acc - mu_c
        var_c = jnp.mean(xc * xc, axis=-1, keepdims=True)
        ln = xc * lax.rsqrt(var_c + LN_EPS) * clg_ref[...] + clb_ref[...]
        out = ln * jax.nn.sigmoid(ln)
        o_ref[j, c * CHUNK:(c + 1) * CHUNK, RW:RW + CW] = out.astype(o_ref.dtype)
        return out[0:SUBLANES, 0:128]

    conv_todo = [(j, c) for c in range(n_chunks) for j in range(SEQS)]

    def tie_conv(x, n_pieces):
        for _ in range(min(n_pieces, len(conv_todo))):
            piece = conv_piece(*conv_todo.pop(0)) * zero_ref[...]
            corner = jnp.concatenate([x[0:SUBLANES, 0:128] + piece, x[0:SUBLANES, 128:]], axis=1)
            x = jnp.concatenate([corner, x[SUBLANES:]], axis=0)
        return x

    for lvl in range(1, 6):
        xs[0] = tie_conv(xs[0], 2 if lvl == 1 else 1)
        join = lvl_ref[lvl - 1]
        ys = [_hmm(x, n_l * join, bd_mask).astype(BF16) for x, n_l in zip(xs, n_ls)]
        xs = [x + _hmm(y, x, bd_mask) for x, y in zip(xs, ys)]
    xs[0] = tie_conv(xs[0], 1)
    t_invs = [x.astype(BF16) for x in xs]

    aks = []
    for rows, sl in items:
        lhs = jnp.concatenate([at_s[rows, sl], r_s[rows, sl].astype(BF16)], axis=0)
        ak = _dot(lhs, _block_diag(kt_s[rows, sl], bd_mask), NT)
        aks.append((jnp.where(strict, ak[:CHUNK], 0.0).astype(BF16),
                    jnp.where(incl, ak[CHUNK:], 0.0).astype(BF16)))
    a_ps = [_dot(t_inv, _block_diag(at_s[rows, sl], bd_mask)).astype(BF16)
            for t_inv, (rows, sl) in zip(t_invs, items)]
    w0s = []
    for (a_ak, a_rk), (rows, sl) in zip(aks, items):
        v_bd = _block_diag(v_s[rows, sl].astype(BF16), bd_mask)
        w0s.append(_dot(a_ak, v_bd).astype(BF16))
        y_s[rows, sl] = _dot(a_rk, v_bd)
    u0s = []
    for i, (rows, sl) in enumerate(items):
        u0s.append(_dot(t_invs[i], _block_diag(w0s[i], bd_mask)))
        ap_s[rows, sl] = a_ps[i]
        rp_s[rows, sl] = (r_s[rows, sl] + _dot(a_rbs[i], _block_diag(a_ps[i], bd_mask))).astype(BF16)
    u0s[0] = tie_conv(u0s[0], len(conv_todo))
    for i, (rows, sl) in enumerate(items):
        u0_s[rows, sl] = u0s[i]
        y_s[rows, sl] = y_s[rows, sl] + _dot(a_rbs[i], _block_diag(u0s[i].astype(BF16), bd_mask))

    rid = lax.broadcasted_iota(jnp.int32, (GROUP, GROUP), 0) // HEAD
    cid = lax.broadcasted_iota(jnp.int32, (GROUP, GROUP), 1) // HEAD
    same_head = rid == cid
    groups = [slice(g * GROUP, (g + 1) * GROUP) for g in range(N_GROUPS)]

    def state_read(j, rows):
        us_parts = []
        for g, sl in enumerate(groups):
            s = s_ref[j, g]
            lhs = jnp.concatenate([ap_s[rows, sl], rp_s[rows, sl]], axis=0)
            us = _dot(lhs, s.astype(BF16), NT)
            y_s[rows, sl] = y_s[rows, sl] + us[CHUNK:]
            us_parts.append((s, us[:CHUNK] + u0_s[rows, sl]))
        return us_parts

    def state_write(j, rows, ge_row, us_parts):
        for g, sl in enumerate(groups):
            s, u = us_parts[g]
            uv_t = jnp.concatenate([u, v_s[rows, sl]], axis=0).T.astype(BF16)
            bk = jnp.concatenate([bh_s[rows, sl], kh_s[rows, sl]], axis=0)
            s_ref[j, g] = jnp.where(same_head, s * ge_s[ge_row, sl] + _dot(uv_t, bk), 0.0)

    def gn_center(rows):
        y = y_s[rows, :]
        return y - _head_sum(y, ones_bd) * (1.0 / HEAD)

    def gn_finish(j, rows, out_rows, yc):
        var = _head_sum(yc * yc, ones_bd) * (1.0 / HEAD)
        yn = yc * lax.rsqrt(var + GN_EPS) * gng_ref[...] + gnb_ref[...]
        o_ref[j, out_rows, 0:RW] = ((yn + bonus_s[rows, :]) * gate_s[rows, :]).astype(o_ref.dtype)

    seqs = range(SEQS)
    first = [slice(j * tt, j * tt + CHUNK) for j in seqs]
    parts = [state_read(j, first[j]) for j in seqs]
    for j in seqs:
        state_write(j, first[j], slice(j * n_chunks * SUBLANES, j * n_chunks * SUBLANES + 1), parts[j])

    def chunk_step(c, carry):
        t0 = pl.multiple_of(c * CHUNK, CHUNK)
        out_prev = pl.ds(pl.multiple_of(t0 - CHUNK, CHUNK), CHUNK)
        rows = [pl.ds(pl.multiple_of(j * tt + t0, CHUNK), CHUNK) for j in seqs]
        prev = [pl.ds(pl.multiple_of(j * tt + t0 - CHUNK, CHUNK), CHUNK) for j in seqs]
        ge_rows = [pl.ds(pl.multiple_of((j * n_chunks + c) * SUBLANES, SUBLANES), 1) for j in seqs]
        parts = [state_read(j, rows[j]) for j in seqs]
        ycs = [gn_center(prev[j]) for j in seqs]
        for j in seqs:
            state_write(j, rows[j], ge_rows[j], parts[j])
        for j in seqs:
            gn_finish(j, prev[j], out_prev, ycs[j])
        return carry

    lax.fori_loop(1, n_chunks, chunk_step, 0)
    last_out = slice(tt - CHUNK, tt)
    for j in seqs:
        last = slice((j + 1) * tt - CHUNK, (j + 1) * tt)
        gn_finish(j, last, last_out, gn_center(last))
        conv_ref[j, 0:CONV_HIST, :] = conv_ref[j, tt:tt + CONV_HIST, :]


def _mixer(x3, params):
    batch, seq, _ = x3.shape
    tt = min(MIX_TILE, seq)
    rows = SEQS * tt
    consts = [_const_spec(a.shape) for a in params]
    return pl.pallas_call(
        _mixer_kernel,
        grid=(batch // SEQS, seq // tt),
        in_specs=[pl.BlockSpec((SEQS, tt, D_MODEL), lambda b, t: (b, t, 0))] + consts,
        out_specs=pl.BlockSpec((SEQS, tt, RW + CW), lambda b, t: (b, t, 0)),
        out_shape=jax.ShapeDtypeStruct((batch, seq, RW + CW), BF16),
        scratch_shapes=[
            pltpu.VMEM((SEQS, N_GROUPS, GROUP, GROUP), F32),
            pltpu.VMEM((SEQS, RWKV_COLS), F32),
            pltpu.VMEM((SEQS, CONV_HIST + tt, CW), F32),
        ] + [pltpu.VMEM((rows, RW), F32) for _ in range(10)]
        + [pltpu.VMEM((rows, RW), BF16) for _ in range(7)]
        + [pltpu.VMEM((rows // CHUNK * SUBLANES, RW), F32),
           pltpu.VMEM((SEQS, SUBLANES - 1, CONV_HIST + tt - SUBLANES, CW), F32)],
        compiler_params=pltpu.CompilerParams(
            dimension_semantics=("arbitrary", "arbitrary"), vmem_limit_bytes=VMEM_LIMIT),
        name="mixer",
    )(x3, *params)


def kernel(x, norm1_g, w_in, mu_shift, w0, w_up, a0, a_up, g_up, k_k, k_a, r_k, gn_g, gn_b,
           dw_w, dw_b, cln_g, cln_b, w_out, norm2_g, w_ff1, w_ff2, final_g):
    batch, seq, d = x.shape
    depth = w_in.shape[0]
    assert d == D_MODEL and seq % CHUNK == 0 and (batch * seq) % ROW_TILE == 0 and batch % SEQS == 0
    x2 = x.reshape(batch * seq, d)

    idx = jnp.arange(GROUP) // HEAD
    ones_bd = (idx[:, None] == idx[None, :]).astype(BF16)
    tri = jnp.arange(CHUNK)
    ltri = (tri[None, :] <= tri[:, None]).astype(BF16)
    zero = jnp.zeros((1, 128), F32)
    pos = tri[:, None] ^ (jnp.arange(GROUP)[None, :] % HEAD)
    joins = jnp.stack([(pos >> lvl) == 1 for lvl in range(1, 6)]).astype(BF16)
    row = lambda a: a.reshape(1, -1)

    for l in range(depth):
        wa = jnp.zeros((DECAY_LORA + ICLR_LORA, 2 * RW), F32)
        wa = wa.at[:DECAY_LORA, :RW].set(w_up[l]).at[DECAY_LORA:, RW:].set(a_up[l]).astype(BF16)
        params = [row(norm1_g[l]), w_in[l].astype(BF16), row(mu_shift[l]), row(w0[l]), row(a0[l]), wa,
                  g_up[l].astype(BF16), row(k_k[l]), row(k_a[l]), row(r_k[l]), row(gn_g[l]), row(gn_b[l]),
                  dw_w[l], row(dw_b[l]), row(cln_g[l]), row(cln_b[l]), ones_bd, ltri, zero, joins]
        mixed = _mixer(x2.reshape(batch, seq, d), params).reshape(batch * seq, RW + CW)
        x2 = _out_ffn(x2, mixed, w_out[l].astype(BF16), row(norm2_g[l]),
                      w_ff1[l].astype(BF16), w_ff2[l].astype(BF16), row(final_g),
                      final=(l == depth - 1))
    return x2.reshape(batch, seq, d)
```

```python
import functools

import jax
import jax.numpy as jnp
from jax import lax
from jax.experimental import pallas as pl
from jax.experimental.pallas import tpu as pltpu

F32 = jnp.float32
BF16 = jnp.bfloat16

D_MODEL = 1024
RW = 512
HEAD = 64
CW = 512
DECAY_LORA = 64
ICLR_LORA = 64
GATE_LORA = 128
CONV_K = 31
D_FF = 4 * D_MODEL
NORM_EPS = 1e-5
LN_EPS = 1e-5
GN_EPS = 64e-5
L2_EPS = 1e-12
DECAY_SCALE = 0.6065306597126334

OFF_K = RW
OFF_V = 2 * RW
OFF_WD = 3 * RW
OFF_GD = OFF_WD + DECAY_LORA + ICLR_LORA
RWKV_COLS = OFF_GD + GATE_LORA
IN_COLS = RWKV_COLS + 2 * CW

CHUNK = 64
GROUP = 256
N_GROUPS = RW // GROUP
HEADS_PER_GROUP = GROUP // HEAD
SUBLANES = 8
CONV_HIST = 32

ROW_TILE = 512
MIX_TILE = 256
SEQS = 2
FF_SPLIT = 4
VMEM_LIMIT = 56 * 1024 * 1024

NN = ((1,), (0,))
NT = ((1,), (1,))


def _dot(a, b, dims=NN):
    return lax.dot_general(a, b, (dims, ((), ())), preferred_element_type=F32)


def _split(x):
    hi = x.astype(BF16)
    lo = (x - hi.astype(F32)).astype(BF16)
    return hi, lo


def _rmsnorm(x, g):
    ms = jnp.mean(x * x, axis=-1, keepdims=True)
    return x * lax.rsqrt(ms + NORM_EPS) * g


def _const_spec(shape):
    return pl.BlockSpec(shape, lambda *_: (0,) * len(shape), pipeline_mode=pl.Buffered(1))


def _outffn_kernel(x_ref, mix_ref, wout_ref, g2_ref, w1_ref, w2_ref, gf_ref, o_ref, *, final):
    x1 = x_ref[...] + _dot(mix_ref[...], wout_ref[...])
    h2 = _rmsnorm(x1, g2_ref[...]).astype(BF16)
    acc = x1
    blk = D_FF // FF_SPLIT
    for j in range(FF_SPLIT):
        hid = _dot(h2, w1_ref[:, j * blk:(j + 1) * blk])
        hid = jnp.square(jnp.maximum(hid, 0.0)).astype(BF16)
        acc = acc + _dot(hid, w2_ref[j * blk:(j + 1) * blk, :])
    if final:
        acc = _rmsnorm(acc, gf_ref[...])
    o_ref[...] = acc


def _out_ffn(x2, mixed, wout_bf, g2, w1_bf, w2_bf, gf, final):
    m = x2.shape[0]
    return pl.pallas_call(
        functools.partial(_outffn_kernel, final=final),
        grid=(m // ROW_TILE,),
        in_specs=[
            pl.BlockSpec((ROW_TILE, D_MODEL), lambda i: (i, 0)),
            pl.BlockSpec((ROW_TILE, D_MODEL), lambda i: (i, 0)),
            _const_spec((D_MODEL, D_MODEL)),
            _const_spec((1, D_MODEL)),
            _const_spec((D_MODEL, D_FF)),
            _const_spec((D_FF, D_MODEL)),
            _const_spec((1, D_MODEL)),
        ],
        out_specs=pl.BlockSpec((ROW_TILE, D_MODEL), lambda i: (i, 0)),
        out_shape=jax.ShapeDtypeStruct((m, D_MODEL), F32),
        compiler_params=pltpu.CompilerParams(
            dimension_semantics=("arbitrary",), vmem_limit_bytes=VMEM_LIMIT),
        name="out_ffn",
    )(x2, mixed, wout_bf, g2, w1_bf, w2_bf, gf)


def _head_sum(x, ones_bd):
    parts = [_dot(x[:, g * GROUP:(g + 1) * GROUP].astype(BF16), ones_bd) for g in range(N_GROUPS)]
    return jnp.concatenate(parts, axis=-1)


def _block_diag(q, bd_mask):
    return jnp.concatenate([q] * HEADS_PER_GROUP, axis=0) * bd_mask


def _hmm(a, b, bd_mask):
    return _dot(a.astype(BF16), _block_diag(b.astype(BF16), bd_mask))


def _mixer_kernel(x_ref, g1_ref, win_ref, mu_ref, w0_ref, a0_ref, wa_ref, gup_ref, kk_ref, ka_ref, rk_ref,
                  gng_ref, gnb_ref, dww_ref, dwb_ref, clg_ref, clb_ref, ones_ref, ltri_ref, zero_ref, lvl_ref,
                  o_ref,
                  s_ref, carry_ref, conv_ref, r_s, k_s, v_s, nkk_s, b_s, lw_s, y_s, u0_s, gate_s, bonus_s,
                  ap_s, rp_s, bh_s, kh_s, at_s, kt_s, bt_s, ge_s, shift_ref):
    tt = x_ref.shape[1]
    t_idx = pl.program_id(1)

    @pl.when(t_idx == 0)
    def _():
        s_ref[...] = jnp.zeros_like(s_ref)
        carry_ref[...] = jnp.zeros_like(carry_ref)
        conv_ref[:, 0:CONV_HIST, :] = jnp.zeros((SEQS, CONV_HIST, CW), F32)

    ones_bd = ones_ref[...]
    bd_mask = ones_bd
    mu = mu_ref[...]
    ltri = ltri_ref[...]
    first_row = lax.broadcasted_iota(jnp.int32, (SUBLANES, 1), 0) == 0
    row = lax.broadcasted_iota(jnp.int32, (CHUNK, GROUP), 0)
    col = lax.broadcasted_iota(jnp.int32, (CHUNK, GROUP), 1) & (HEAD - 1)
    strict = col < row
    incl = col <= row
    diff = row ^ col
    chunk_row = lax.broadcasted_iota(jnp.int32, (CHUNK, 1), 0)
    n_chunks = tt // CHUNK
    span = tt + CONV_HIST - SUBLANES

    def projections(j):
        h = _rmsnorm(x_ref[j], g1_ref[...]).astype(BF16)

        def shifted(lo, hi):
            cur = _dot(h, win_ref[:, lo:hi])
            prev = pltpu.roll(cur, 1, axis=0)
            head = jnp.where(first_row, carry_ref[j:j + 1, lo:hi], prev[0:SUBLANES, :])
            carry_ref[j:j + 1, lo:hi] = cur[tt - 1:tt, :]
            prev = jnp.concatenate([head, prev[SUBLANES:, :]], axis=0)
            return cur + (prev - cur) * mu[:, lo:hi]

        parts = (shifted(0, OFF_K), shifted(OFF_K, OFF_V), shifted(OFF_V, OFF_WD),
                 shifted(OFF_WD, OFF_GD), shifted(OFF_GD, RWKV_COLS))
        glu = (_dot(h, win_ref[:, RWKV_COLS:RWKV_COLS + CW])
               * jax.nn.sigmoid(_dot(h, win_ref[:, RWKV_COLS + CW:IN_COLS])))
        conv_ref[j, CONV_HIST:CONV_HIST + tt, :] = glu
        for ph in range(1, SUBLANES):
            shift_ref[j, ph - 1] = conv_ref[j, ph:ph + span, :]
        return parts

    def token_prep(j, r, k, v, wa, gd):
        rows = slice(j * tt, (j + 1) * tt)
        lane = lax.broadcasted_iota(jnp.int32, wa.shape, 1)
        wa = jnp.where(lane < DECAY_LORA, jnp.tanh(wa), wa)
        lora = _dot(wa.astype(BF16), wa_ref[...])
        z = w0_ref[...] + lora[:, :RW]
        lw_s[rows, :] = jax.nn.sigmoid(z) * (-DECAY_SCALE)
        a = jax.nn.sigmoid(a0_ref[...] + lora[:, RW:])
        gate_s[rows, :] = _dot(jax.nn.sigmoid(gd).astype(BF16), gup_ref[...])
        kk = k * kk_ref[...]
        kk = kk * jnp.minimum(lax.rsqrt(_head_sum(kk * kk, ones_bd)), 1.0 / L2_EPS)
        k = k * (1.0 + (a - 1.0) * ka_ref[...])
        bonus_s[rows, :] = _head_sum(r * k * rk_ref[...], ones_bd) * v
        r_s[rows, :] = r
        k_s[rows, :] = k
        v_s[rows, :] = v
        nkk_s[rows, :] = -kk
        b_s[rows, :] = kk * a

    def items_of(j):
        return [(slice((j * n_chunks + c) * CHUNK, (j * n_chunks + c + 1) * CHUNK),
                 slice(g * GROUP, (g + 1) * GROUP)) for c in range(n_chunks) for g in range(N_GROUPS)]

    def decayed_operands(j):
        for c in range(j * n_chunks, (j + 1) * n_chunks):
            rows = slice(c * CHUNK, (c + 1) * CHUNK)
            lw = lw_s[rows, :]
            lw_hi, lw_lo = _split(lw)
            cum = _dot(ltri, lw_hi) + _dot(ltri, lw_lo)
            e_pos = jnp.exp(cum)
            e_neg = 1.0 / e_pos
            g_end = e_pos[CHUNK - 1:CHUNK, :]
            e_rem = g_end * e_neg
            k_c = k_s[rows, :]
            b_c = b_s[rows, :]
            r_s[rows, :] = r_s[rows, :] * e_pos
            e_prev = jnp.where(chunk_row == 0, 1.0, pltpu.roll(e_pos, 1, axis=0))
            at_s[rows, :] = (nkk_s[rows, :] * e_prev).astype(BF16)
            kt_s[rows, :] = (k_c * e_neg).astype(BF16)
            bt_s[rows, :] = (b_c * e_neg).astype(BF16)
            kh_s[rows, :] = (k_c * e_rem).astype(BF16)
            bh_s[rows, :] = (b_c * e_rem).astype(BF16)
            ge_s[c * SUBLANES:(c + 1) * SUBLANES, :] = jnp.broadcast_to(g_end, (SUBLANES, RW))

    def first_products(j):
        n_ls, xs, a_rbs = [], [], []
        for rows, sl in items_of(j):
            lhs = jnp.concatenate([at_s[rows, sl], r_s[rows, sl].astype(BF16)], axis=0)
            ab = _dot(lhs, _block_diag(bt_s[rows, sl], bd_mask), NT)
            n_l = jnp.where(strict, ab[:CHUNK], 0.0)
            n_ls.append(n_l.astype(BF16))
            xs.append(jnp.where(diff == 0, 1.0, 0.0) + jnp.where(diff == 1, n_l, 0.0))
            a_rbs.append(jnp.where(incl, ab[CHUNK:], 0.0).astype(BF16))
        return n_ls, xs, a_rbs

    def conv_piece(j, c):
        acc = jnp.zeros((CHUNK, CW), F32) + dwb_ref[...]
        for tap in range(CONV_K):
            off = CONV_HIST - (CONV_K - 1) + tap + c * CHUNK
            ph = off % SUBLANES
            src = conv_ref.at[j] if ph == 0 else shift_ref.at[j, ph - 1]
            acc = acc + src[off - ph:off - ph + CHUNK, :] * dww_ref[tap:tap + 1, :]
        mu_c = jnp.mean(acc, axis=-1, keepdims=True)
        xc = acc - mu_c
        var_c = jnp.mean(xc * xc, axis=-1, keepdims=True)
        ln = xc * lax.rsqrt(var_c + LN_EPS) * clg_ref[...] + clb_ref[...]
        out = ln * jax.nn.sigmoid(ln)
        o_ref[j, c * CHUNK:(c + 1) * CHUNK, RW:RW + CW] = out.astype(o_ref.dtype)
        return out[0:SUBLANES, 0:128]

    def tie_conv(x, j, c):
        piece = conv_piece(j, c) * zero_ref[...]
        corner = jnp.concatenate([x[0:SUBLANES, 0:128] + piece, x[0:SUBLANES, 128:]], axis=1)
        return jnp.concatenate([corner, x[SUBLANES:]], axis=0)

    def inverse_level(j, lvl, n_ls, xs):
        if lvl <= n_chunks:
            xs[0] = tie_conv(xs[0], j, lvl - 1)
        join = lvl_ref[lvl - 1]
        ys = [_hmm(x, n_l * join, bd_mask).astype(BF16) for x, n_l in zip(xs, n_ls)]
        return [x + _hmm(y, x, bd_mask) for x, y in zip(xs, ys)]

    def solve_products(j, xs, a_rbs):
        items = items_of(j)
        t_invs = [x.astype(BF16) for x in xs]
        aks = []
        for rows, sl in items:
            lhs = jnp.concatenate([at_s[rows, sl], r_s[rows, sl].astype(BF16)], axis=0)
            ak = _dot(lhs, _block_diag(kt_s[rows, sl], bd_mask), NT)
            aks.append((jnp.where(strict, ak[:CHUNK], 0.0).astype(BF16),
                        jnp.where(incl, ak[CHUNK:], 0.0).astype(BF16)))
        a_ps = [_dot(t_inv, _block_diag(at_s[rows, sl], bd_mask)).astype(BF16)
                for t_inv, (rows, sl) in zip(t_invs, items)]
        w0s = []
        for (a_ak, a_rk), (rows, sl) in zip(aks, items):
            v_bd = _block_diag(v_s[rows, sl].astype(BF16), bd_mask)
            w0s.append(_dot(a_ak, v_bd).astype(BF16))
            y_s[rows, sl] = _dot(a_rk, v_bd)
        u0s = []
        for i, (rows, sl) in enumerate(items):
            u0s.append(_dot(t_invs[i], _block_diag(w0s[i], bd_mask)))
            ap_s[rows, sl] = a_ps[i]
            rp_s[rows, sl] = (r_s[rows, sl] + _dot(a_rbs[i], _block_diag(a_ps[i], bd_mask))).astype(BF16)
        for i, (rows, sl) in enumerate(items):
            u0_s[rows, sl] = u0s[i]
            y_s[rows, sl] = y_s[rows, sl] + _dot(a_rbs[i], _block_diag(u0s[i].astype(BF16), bd_mask))

    assert SEQS == 2 and n_chunks <= 5
    proj0 = projections(0)
    proj1 = projections(1)
    token_prep(0, *proj0)
    decayed_operands(0)
    n0, x0, arb0 = first_products(0)
    x0 = inverse_level(0, 1, n0, x0)
    token_prep(1, *proj1)
    x0 = inverse_level(0, 2, n0, x0)
    decayed_operands(1)
    x0 = inverse_level(0, 3, n0, x0)
    n1, x1, arb1 = first_products(1)
    x0 = inverse_level(0, 4, n0, x0)
    x0 = inverse_level(0, 5, n0, x0)
    solve_products(0, x0, arb0)
    for lvl in range(1, 6):
        x1 = inverse_level(1, lvl, n1, x1)
    solve_products(1, x1, arb1)

    rid = lax.broadcasted_iota(jnp.int32, (GROUP, GROUP), 0) // HEAD
    cid = lax.broadcasted_iota(jnp.int32, (GROUP, GROUP), 1) // HEAD
    same_head = rid == cid
    groups = [slice(g * GROUP, (g + 1) * GROUP) for g in range(N_GROUPS)]

    def state_read(j, rows):
        us_parts = []
        for g, sl in enumerate(groups):
            s = s_ref[j, g]
            lhs = jnp.concatenate([ap_s[rows, sl], rp_s[rows, sl]], axis=0)
            us = _dot(lhs, s.astype(BF16), NT)
            y_s[rows, sl] = y_s[rows, sl] + us[CHUNK:]
            us_parts.append((s, us[:CHUNK] + u0_s[rows, sl]))
        return us_parts

    def state_write(j, rows, ge_row, us_parts):
        for g, sl in enumerate(groups):
            s, u = us_parts[g]
            uv_t = jnp.concatenate([u, v_s[rows, sl]], axis=0).T.astype(BF16)
            bk = jnp.concatenate([bh_s[rows, sl], kh_s[rows, sl]], axis=0)
            s_ref[j, g] = jnp.where(same_head, s * ge_s[ge_row, sl] + _dot(uv_t, bk), 0.0)

    def gn_center(rows):
        y = y_s[rows, :]
        return y - _head_sum(y, ones_bd) * (1.0 / HEAD)

    def gn_finish(j, rows, out_rows, yc):
        var = _head_sum(yc * yc, ones_bd) * (1.0 / HEAD)
        yn = yc * lax.rsqrt(var + GN_EPS) * gng_ref[...] + gnb_ref[...]
        o_ref[j, out_rows, 0:RW] = ((yn + bonus_s[rows, :]) * gate_s[rows, :]).astype(o_ref.dtype)

    seqs = range(SEQS)
    first = [slice(j * tt, j * tt + CHUNK) for j in seqs]
    parts = [state_read(j, first[j]) for j in seqs]
    for j in seqs:
        state_write(j, first[j], slice(j * n_chunks * SUBLANES, j * n_chunks * SUBLANES + 1), parts[j])

    def chunk_step(c, carry):
        t0 = pl.multiple_of(c * CHUNK, CHUNK)
        out_prev = pl.ds(pl.multiple_of(t0 - CHUNK, CHUNK), CHUNK)
        rows = [pl.ds(pl.multiple_of(j * tt + t0, CHUNK), CHUNK) for j in seqs]
        prev = [pl.ds(pl.multiple_of(j * tt + t0 - CHUNK, CHUNK), CHUNK) for j in seqs]
        ge_rows = [pl.ds(pl.multiple_of((j * n_chunks + c) * SUBLANES, SUBLANES), 1) for j in seqs]
        parts = [state_read(j, rows[j]) for j in seqs]
        ycs = [gn_center(prev[j]) for j in seqs]
        for j in seqs:
            state_write(j, rows[j], ge_rows[j], parts[j])
        for j in seqs:
            gn_finish(j, prev[j], out_prev, ycs[j])
        return carry

    lax.fori_loop(1, n_chunks, chunk_step, 0)
    last_out = slice(tt - CHUNK, tt)
    for j in seqs:
        last = slice((j + 1) * tt - CHUNK, (j + 1) * tt)
        gn_finish(j, last, last_out, gn_center(last))
        conv_ref[j, 0:CONV_HIST, :] = conv_ref[j, tt:tt + CONV_HIST, :]


def _mixer(x3, params):
    batch, seq, _ = x3.shape
    tt = min(MIX_TILE, seq)
    rows = SEQS * tt
    consts = [_const_spec(a.shape) for a in params]
    return pl.pallas_call(
        _mixer_kernel,
        grid=(batch // SEQS, seq // tt),
        in_specs=[pl.BlockSpec((SEQS, tt, D_MODEL), lambda b, t: (b, t, 0))] + consts,
        out_specs=pl.BlockSpec((SEQS, tt, RW + CW), lambda b, t: (b, t, 0)),
        out_shape=jax.ShapeDtypeStruct((batch, seq, RW + CW), BF16),
        scratch_shapes=[
            pltpu.VMEM((SEQS, N_GROUPS, GROUP, GROUP), F32),
            pltpu.VMEM((SEQS, RWKV_COLS), F32),
            pltpu.VMEM((SEQS, CONV_HIST + tt, CW), F32),
        ] + [pltpu.VMEM((rows, RW), F32) for _ in range(10)]
        + [pltpu.VMEM((rows, RW), BF16) for _ in range(7)]
        + [pltpu.VMEM((rows // CHUNK * SUBLANES, RW), F32),
           pltpu.VMEM((SEQS, SUBLANES - 1, CONV_HIST + tt - SUBLANES, CW), F32)],
        compiler_params=pltpu.CompilerParams(
            dimension_semantics=("arbitrary", "arbitrary"), vmem_limit_bytes=VMEM_LIMIT),
        name="mixer",
    )(x3, *params)


def kernel(x, norm1_g, w_in, mu_shift, w0, w_up, a0, a_up, g_up, k_k, k_a, r_k, gn_g, gn_b,
           dw_w, dw_b, cln_g, cln_b, w_out, norm2_g, w_ff1, w_ff2, final_g):
    batch, seq, d = x.shape
    depth = w_in.shape[0]
    assert d == D_MODEL and seq % CHUNK == 0 and (batch * seq) % ROW_TILE == 0 and batch % SEQS == 0
    x2 = x.reshape(batch * seq, d)

    idx = jnp.arange(GROUP) // HEAD
    ones_bd = (idx[:, None] == idx[None, :]).astype(BF16)
    tri = jnp.arange(CHUNK)
    ltri = (tri[None, :] <= tri[:, None]).astype(BF16)
    zero = jnp.zeros((1, 128), F32)
    pos = tri[:, None] ^ (jnp.arange(GROUP)[None, :] % HEAD)
    joins = jnp.stack([(pos >> lvl) == 1 for lvl in range(1, 6)]).astype(BF16)
    row = lambda a: a.reshape(1, -1)

    for l in range(depth):
        wa = jnp.zeros((DECAY_LORA + ICLR_LORA, 2 * RW), F32)
        wa = wa.at[:DECAY_LORA, :RW].set(w_up[l]).at[DECAY_LORA:, RW:].set(a_up[l]).astype(BF16)
        params = [row(norm1_g[l]), w_in[l].astype(BF16), row(mu_shift[l]), row(w0[l]), row(a0[l]), wa,
                  g_up[l].astype(BF16), row(k_k[l]), row(k_a[l]), row(r_k[l]), row(gn_g[l]), row(gn_b[l]),
                  dw_w[l], row(dw_b[l]), row(cln_g[l]), row(cln_b[l]), ones_bd, ltri, zero, joins]
        mixed = _mixer(x2.reshape(batch, seq, d), params).reshape(batch * seq, RW + CW)
        x2 = _out_ffn(x2, mixed, w_out[l].astype(BF16), row(norm2_g[l]),
                      w_ff1[l].astype(BF16), w_ff2[l].astype(BF16), row(final_g),
                      final=(l == depth - 1))
    return x2.reshape(batch, seq, d)
```

```python
import functools

import jax
import jax.numpy as jnp
from jax import lax
from jax.experimental import pallas as pl
from jax.experimental.pallas import tpu as pltpu

F32 = jnp.float32
BF16 = jnp.bfloat16

D_MODEL = 1024
RW = 512
HEAD = 64
CW = 512
DECAY_LORA = 64
ICLR_LORA = 64
GATE_LORA = 128
CONV_K = 31
D_FF = 4 * D_MODEL
NORM_EPS = 1e-5
LN_EPS = 1e-5
GN_EPS = 64e-5
L2_EPS = 1e-12
DECAY_SCALE = 0.6065306597126334

OFF_K = RW
OFF_V = 2 * RW
OFF_WD = 3 * RW
OFF_GD = OFF_WD + DECAY_LORA + ICLR_LORA
RWKV_COLS = OFF_GD + GATE_LORA
IN_COLS = RWKV_COLS + 2 * CW

CHUNK = 64
GROUP = 256
N_GROUPS = RW // GROUP
HEADS_PER_GROUP = GROUP // HEAD
SUBLANES = 8
CONV_HIST = 32

ROW_TILE = 512
MIX_TILE = 256
SEQS = 2
FF_SPLIT = 8
VMEM_LIMIT = 56 * 1024 * 1024

NN = ((1,), (0,))
NT = ((1,), (1,))


def _dot(a, b, dims=NN):
    return lax.dot_general(a, b, (dims, ((), ())), preferred_element_type=F32)


def _split(x):
    hi = x.astype(BF16)
    lo = (x - hi.astype(F32)).astype(BF16)
    return hi, lo


def _rmsnorm(x, g):
    ms = jnp.mean(x * x, axis=-1, keepdims=True)
    return x * lax.rsqrt(ms + NORM_EPS) * g


def _const_spec(shape):
    return pl.BlockSpec(shape, lambda *_: (0,) * len(shape), pipeline_mode=pl.Buffered(1))


def _outffn_kernel(x_ref, mix_ref, wout_ref, g2_ref, w1_ref, w2_ref, gf_ref, o_ref, *, final):
    x1 = x_ref[...] + _dot(mix_ref[...], wout_ref[...])
    h2 = _rmsnorm(x1, g2_ref[...]).astype(BF16)
    acc = x1
    blk = D_FF // FF_SPLIT
    for j in range(FF_SPLIT):
        hid = _dot(h2, w1_ref[:, j * blk:(j + 1) * blk])
        hid = jnp.square(jnp.maximum(hid, 0.0)).astype(BF16)
        acc = acc + _dot(hid, w2_ref[j * blk:(j + 1) * blk, :])
    if final:
        acc = _rmsnorm(acc, gf_ref[...])
    o_ref[...] = acc


def _out_ffn(x2, mixed, wout_bf, g2, w1_bf, w2_bf, gf, final):
    m = x2.shape[0]
    return pl.pallas_call(
        functools.partial(_outffn_kernel, final=final),
        grid=(m // ROW_TILE,),
        in_specs=[
            pl.BlockSpec((ROW_TILE, D_MODEL), lambda i: (i, 0)),
            pl.BlockSpec((ROW_TILE, D_MODEL), lambda i: (i, 0)),
            _const_spec((D_MODEL, D_MODEL)),
            _const_spec((1, D_MODEL)),
            _const_spec((D_MODEL, D_FF)),
            _const_spec((D_FF, D_MODEL)),
            _const_spec((1, D_MODEL)),
        ],
        out_specs=pl.BlockSpec((ROW_TILE, D_MODEL), lambda i: (i, 0)),
        out_shape=jax.ShapeDtypeStruct((m, D_MODEL), F32),
        compiler_params=pltpu.CompilerParams(
            dimension_semantics=("arbitrary",), vmem_limit_bytes=VMEM_LIMIT),
        name="out_ffn",
    )(x2, mixed, wout_bf, g2, w1_bf, w2_bf, gf)


def _head_sum(x, ones_bd):
    parts = [_dot(x[:, g * GROUP:(g + 1) * GROUP].astype(BF16), ones_bd) for g in range(N_GROUPS)]
    return jnp.concatenate(parts, axis=-1)


def _block_diag(q, bd_mask):
    return jnp.concatenate([q] * HEADS_PER_GROUP, axis=0) * bd_mask


def _hmm(a, b, bd_mask):
    return _dot(a.astype(BF16), _block_diag(b.astype(BF16), bd_mask))


def _mixer_kernel(x_ref, g1_ref, win_ref, mu_ref, w0_ref, a0_ref, wa_ref, gup_ref, kk_ref, ka_ref, rk_ref,
                  gng_ref, gnb_ref, dww_ref, dwb_ref, clg_ref, clb_ref, ones_ref, ltri_ref, zero_ref, lvl_ref,
                  o_ref,
                  s_ref, carry_ref, conv_ref, r_s, k_s, v_s, nkk_s, b_s, lw_s, y_s, u0_s, gate_s, bonus_s,
                  ap_s, rp_s, bh_s, kh_s, at_s, kt_s, bt_s, ge_s, shift_ref):
    tt = x_ref.shape[1]
    t_idx = pl.program_id(1)

    @pl.when(t_idx == 0)
    def _():
        s_ref[...] = jnp.zeros_like(s_ref)
        carry_ref[...] = jnp.zeros_like(carry_ref)
        conv_ref[:, 0:CONV_HIST, :] = jnp.zeros((SEQS, CONV_HIST, CW), F32)

    ones_bd = ones_ref[...]
    bd_mask = ones_bd
    mu = mu_ref[...]
    ltri = ltri_ref[...]
    first_row = lax.broadcasted_iota(jnp.int32, (SUBLANES, 1), 0) == 0
    row = lax.broadcasted_iota(jnp.int32, (CHUNK, GROUP), 0)
    col = lax.broadcasted_iota(jnp.int32, (CHUNK, GROUP), 1) & (HEAD - 1)
    strict = col < row
    incl = col <= row
    diff = row ^ col
    chunk_row = lax.broadcasted_iota(jnp.int32, (CHUNK, 1), 0)
    n_chunks = tt // CHUNK
    span = tt + CONV_HIST - SUBLANES

    def projections(j):
        h = _rmsnorm(x_ref[j], g1_ref[...]).astype(BF16)

        def shifted(lo, hi):
            cur = _dot(h, win_ref[:, lo:hi])
            prev = pltpu.roll(cur, 1, axis=0)
            head = jnp.where(first_row, carry_ref[j:j + 1, lo:hi], prev[0:SUBLANES, :])
            carry_ref[j:j + 1, lo:hi] = cur[tt - 1:tt, :]
            prev = jnp.concatenate([head, prev[SUBLANES:, :]], axis=0)
            return cur + (prev - cur) * mu[:, lo:hi]

        parts = (shifted(0, OFF_K), shifted(OFF_K, OFF_V), shifted(OFF_V, OFF_WD),
                 shifted(OFF_WD, OFF_GD), shifted(OFF_GD, RWKV_COLS))
        glu = (_dot(h, win_ref[:, RWKV_COLS:RWKV_COLS + CW])
               * jax.nn.sigmoid(_dot(h, win_ref[:, RWKV_COLS + CW:IN_COLS])))
        conv_ref[j, CONV_HIST:CONV_HIST + tt, :] = glu
        for ph in range(1, SUBLANES):
            shift_ref[j, ph - 1] = conv_ref[j, ph:ph + span, :]
        return parts

    def token_prep(j, r, k, v, wa, gd):
        rows = slice(j * tt, (j + 1) * tt)
        lane = lax.broadcasted_iota(jnp.int32, wa.shape, 1)
        wa = jnp.where(lane < DECAY_LORA, jnp.tanh(wa), wa)
        lora = _dot(wa.astype(BF16), wa_ref[...])
        z = w0_ref[...] + lora[:, :RW]
        lw_s[rows, :] = jax.nn.sigmoid(z) * (-DECAY_SCALE)
        a = jax.nn.sigmoid(a0_ref[...] + lora[:, RW:])
        gate_s[rows, :] = _dot(jax.nn.sigmoid(gd).astype(BF16), gup_ref[...])
        kk = k * kk_ref[...]
        kk = kk * jnp.minimum(lax.rsqrt(_head_sum(kk * kk, ones_bd)), 1.0 / L2_EPS)
        k = k * (1.0 + (a - 1.0) * ka_ref[...])
        bonus_s[rows, :] = _head_sum(r * k * rk_ref[...], ones_bd) * v
        r_s[rows, :] = r
        k_s[rows, :] = k
        v_s[rows, :] = v
        nkk_s[rows, :] = -kk
        b_s[rows, :] = kk * a

    def items_of(j):
        return [(slice((j * n_chunks + c) * CHUNK, (j * n_chunks + c + 1) * CHUNK),
                 slice(g * GROUP, (g + 1) * GROUP)) for c in range(n_chunks) for g in range(N_GROUPS)]

    def decayed_operands(j):
        for c in range(j * n_chunks, (j + 1) * n_chunks):
            rows = slice(c * CHUNK, (c + 1) * CHUNK)
            lw = lw_s[rows, :]
            lw_hi, lw_lo = _split(lw)
            cum = _dot(ltri, lw_hi) + _dot(ltri, lw_lo)
            e_pos = jnp.exp(cum)
            e_neg = 1.0 / e_pos
            g_end = e_pos[CHUNK - 1:CHUNK, :]
            e_rem = g_end * e_neg
            k_c = k_s[rows, :]
            b_c = b_s[rows, :]
            r_s[rows, :] = r_s[rows, :] * e_pos
            e_prev = jnp.where(chunk_row == 0, 1.0, pltpu.roll(e_pos, 1, axis=0))
            at_s[rows, :] = (nkk_s[rows, :] * e_prev).astype(BF16)
            kt_s[rows, :] = (k_c * e_neg).astype(BF16)
            bt_s[rows, :] = (b_c * e_neg).astype(BF16)
            kh_s[rows, :] = (k_c * e_rem).astype(BF16)
            bh_s[rows, :] = (b_c * e_rem).astype(BF16)
            ge_s[c * SUBLANES:(c + 1) * SUBLANES, :] = jnp.broadcast_to(g_end, (SUBLANES, RW))

    def first_products(j):
        n_ls, xs, a_rbs = [], [], []
        for rows, sl in items_of(j):
            lhs = jnp.concatenate([at_s[rows, sl], r_s[rows, sl].astype(BF16)], axis=0)
            ab = _dot(lhs, _block_diag(bt_s[rows, sl], bd_mask), NT)
            n_l = jnp.where(strict, ab[:CHUNK], 0.0)
            n_ls.append(n_l.astype(BF16))
            xs.append(jnp.where(diff == 0, 1.0, 0.0) + jnp.where(diff == 1, n_l, 0.0))
            a_rbs.append(jnp.where(incl, ab[CHUNK:], 0.0).astype(BF16))
        return n_ls, xs, a_rbs

    def conv_piece(j, c):
        acc = jnp.zeros((CHUNK, CW), F32) + dwb_ref[...]
        for tap in range(CONV_K):
            off = CONV_HIST - (CONV_K - 1) + tap + c * CHUNK
            ph = off % SUBLANES
            src = conv_ref.at[j] if ph == 0 else shift_ref.at[j, ph - 1]
            acc = acc + src[off - ph:off - ph + CHUNK, :] * dww_ref[tap:tap + 1, :]
        mu_c = jnp.mean(acc, axis=-1, keepdims=True)
        xc = acc - mu_c
        var_c = jnp.mean(xc * xc, axis=-1, keepdims=True)
        ln = xc * lax.rsqrt(var_c + LN_EPS) * clg_ref[...] + clb_ref[...]
        out = ln * jax.nn.sigmoid(ln)
        o_ref[j, c * CHUNK:(c + 1) * CHUNK, RW:RW + CW] = out.astype(o_ref.dtype)
        return out[0:SUBLANES, 0:128]

    def tie_conv(x, j, c):
        piece = conv_piece(j, c) * zero_ref[...]
        corner = jnp.concatenate([x[0:SUBLANES, 0:128] + piece, x[0:SUBLANES, 128:]], axis=1)
        return jnp.concatenate([corner, x[SUBLANES:]], axis=0)

    def inverse_level(j, lvl, n_ls, xs):
        if lvl <= n_chunks:
            xs[0] = tie_conv(xs[0], j, lvl - 1)
        join = lvl_ref[lvl - 1]
        ys = [_hmm(x, n_l * join, bd_mask).astype(BF16) for x, n_l in zip(xs, n_ls)]
        return [x + _hmm(y, x, bd_mask) for x, y in zip(xs, ys)]

    def solve_products(j, xs, a_rbs):
        items = items_of(j)
        t_invs = [x.astype(BF16) for x in xs]
        aks = []
        for rows, sl in items:
            lhs = jnp.concatenate([at_s[rows, sl], r_s[rows, sl].astype(BF16)], axis=0)
            ak = _dot(lhs, _block_diag(kt_s[rows, sl], bd_mask), NT)
            aks.append((jnp.where(strict, ak[:CHUNK], 0.0).astype(BF16),
                        jnp.where(incl, ak[CHUNK:], 0.0).astype(BF16)))
        a_ps = [_dot(t_inv, _block_diag(at_s[rows, sl], bd_mask)).astype(BF16)
                for t_inv, (rows, sl) in zip(t_invs, items)]
        w0s = []
        for (a_ak, a_rk), (rows, sl) in zip(aks, items):
            v_bd = _block_diag(v_s[rows, sl].astype(BF16), bd_mask)
            w0s.append(_dot(a_ak, v_bd).astype(BF16))
            y_s[rows, sl] = _dot(a_rk, v_bd)
        u0s = []
        for i, (rows, sl) in enumerate(items):
            u0s.append(_dot(t_invs[i], _block_diag(w0s[i], bd_mask)))
            ap_s[rows, sl] = a_ps[i]
            rp_s[rows, sl] = (r_s[rows, sl] + _dot(a_rbs[i], _block_diag(a_ps[i], bd_mask))).astype(BF16)
        for i, (rows, sl) in enumerate(items):
            u0_s[rows, sl] = u0s[i]
            y_s[rows, sl] = y_s[rows, sl] + _dot(a_rbs[i], _block_diag(u0s[i].astype(BF16), bd_mask))

    assert SEQS == 2 and n_chunks <= 5
    proj0 = projections(0)
    proj1 = projections(1)
    token_prep(0, *proj0)
    decayed_operands(0)
    n0, x0, arb0 = first_products(0)
    x0 = inverse_level(0, 1, n0, x0)
    token_prep(1, *proj1)
    x0 = inverse_level(0, 2, n0, x0)
    decayed_operands(1)
    x0 = inverse_level(0, 3, n0, x0)
    n1, x1, arb1 = first_products(1)
    x0 = inverse_level(0, 4, n0, x0)
    x0 = inverse_level(0, 5, n0, x0)
    solve_products(0, x0, arb0)
    for lvl in range(1, 6):
        x1 = inverse_level(1, lvl, n1, x1)
    solve_products(1, x1, arb1)

    rid = lax.broadcasted_iota(jnp.int32, (GROUP, GROUP), 0) // HEAD
    cid = lax.broadcasted_iota(jnp.int32, (GROUP, GROUP), 1) // HEAD
    same_head = rid == cid
    groups = [slice(g * GROUP, (g + 1) * GROUP) for g in range(N_GROUPS)]

    def state_read(j, rows):
        us_parts = []
        for g, sl in enumerate(groups):
            s = s_ref[j, g]
            lhs = jnp.concatenate([ap_s[rows, sl], rp_s[rows, sl]], axis=0)
            us = _dot(lhs, s.astype(BF16), NT)
            y_s[rows, sl] = y_s[rows, sl] + us[CHUNK:]
            us_parts.append((s, us[:CHUNK] + u0_s[rows, sl]))
        return us_parts

    def state_write(j, rows, ge_row, us_parts):
        for g, sl in enumerate(groups):
            s, u = us_parts[g]
            uv_t = jnp.concatenate([u, v_s[rows, sl]], axis=0).T.astype(BF16)
            bk = jnp.concatenate([bh_s[rows, sl], kh_s[rows, sl]], axis=0)
            s_ref[j, g] = jnp.where(same_head, s * ge_s[ge_row, sl] + _dot(uv_t, bk), 0.0)

    def gn_center(rows):
        y = y_s[rows, :]
        return y - _head_sum(y, ones_bd) * (1.0 / HEAD)

    def gn_finish(j, rows, out_rows, yc):
        var = _head_sum(yc * yc, ones_bd) * (1.0 / HEAD)
        yn = yc * lax.rsqrt(var + GN_EPS) * gng_ref[...] + gnb_ref[...]
        o_ref[j, out_rows, 0:RW] = ((yn + bonus_s[rows, :]) * gate_s[rows, :]).astype(o_ref.dtype)

    seqs = range(SEQS)
    first = [slice(j * tt, j * tt + CHUNK) for j in seqs]
    parts = [state_read(j, first[j]) for j in seqs]
    for j in seqs:
        state_write(j, first[j], slice(j * n_chunks * SUBLANES, j * n_chunks * SUBLANES + 1), parts[j])

    def chunk_step(c, carry):
        t0 = pl.multiple_of(c * CHUNK, CHUNK)
        out_prev = pl.ds(pl.multiple_of(t0 - CHUNK, CHUNK), CHUNK)
        rows = [pl.ds(pl.multiple_of(j * tt + t0, CHUNK), CHUNK) for j in seqs]
        prev = [pl.ds(pl.multiple_of(j * tt + t0 - CHUNK, CHUNK), CHUNK) for j in seqs]
        ge_rows = [pl.ds(pl.multiple_of((j * n_chunks + c) * SUBLANES, SUBLANES), 1) for j in seqs]
        parts = [state_read(j, rows[j]) for j in seqs]
        ycs = [gn_center(prev[j]) for j in seqs]
        for j in seqs:
            state_write(j, rows[j], ge_rows[j], parts[j])
        for j in seqs:
            gn_finish(j, prev[j], out_prev, ycs[j])
        return carry

    lax.fori_loop(1, n_chunks, chunk_step, 0)
    last_out = slice(tt - CHUNK, tt)
    for j in seqs:
        last = slice((j + 1) * tt - CHUNK, (j + 1) * tt)
        gn_finish(j, last, last_out, gn_center(last))
        conv_ref[j, 0:CONV_HIST, :] = conv_ref[j, tt:tt + CONV_HIST, :]


def _mixer(x3, params):
    batch, seq, _ = x3.shape
    tt = min(MIX_TILE, seq)
    rows = SEQS * tt
    consts = [_const_spec(a.shape) for a in params]
    return pl.pallas_call(
        _mixer_kernel,
        grid=(batch // SEQS, seq // tt),
        in_specs=[pl.BlockSpec((SEQS, tt, D_MODEL), lambda b, t: (b, t, 0))] + consts,
        out_specs=pl.BlockSpec((SEQS, tt, RW + CW), lambda b, t: (b, t, 0)),
        out_shape=jax.ShapeDtypeStruct((batch, seq, RW + CW), BF16),
        scratch_shapes=[
            pltpu.VMEM((SEQS, N_GROUPS, GROUP, GROUP), F32),
            pltpu.VMEM((SEQS, RWKV_COLS), F32),
            pltpu.VMEM((SEQS, CONV_HIST + tt, CW), F32),
        ] + [pltpu.VMEM((rows, RW), F32) for _ in range(10)]
        + [pltpu.VMEM((rows, RW), BF16) for _ in range(7)]
        + [pltpu.VMEM((rows // CHUNK * SUBLANES, RW), F32),
           pltpu.VMEM((SEQS, SUBLANES - 1, CONV_HIST + tt - SUBLANES, CW), F32)],
        compiler_params=pltpu.CompilerParams(
            dimension_semantics=("arbitrary", "arbitrary"), vmem_limit_bytes=VMEM_LIMIT),
        name="mixer",
    )(x3, *params)


def kernel(x, norm1_g, w_in, mu_shift, w0, w_up, a0, a_up, g_up, k_k, k_a, r_k, gn_g, gn_b,
           dw_w, dw_b, cln_g, cln_b, w_out, norm2_g, w_ff1, w_ff2, final_g):
    batch, seq, d = x.shape
    depth = w_in.shape[0]
    assert d == D_MODEL and seq % CHUNK == 0 and (batch * seq) % ROW_TILE == 0 and batch % SEQS == 0
    x2 = x.reshape(batch * seq, d)

    idx = jnp.arange(GROUP) // HEAD
    ones_bd = (idx[:, None] == idx[None, :]).astype(BF16)
    tri = jnp.arange(CHUNK)
    ltri = (tri[None, :] <= tri[:, None]).astype(BF16)
    zero = jnp.zeros((1, 128), F32)
    pos = tri[:, None] ^ (jnp.arange(GROUP)[None, :] % HEAD)
    joins = jnp.stack([(pos >> lvl) == 1 for lvl in range(1, 6)]).astype(BF16)
    row = lambda a: a.reshape(1, -1)

    for l in range(depth):
        wa = jnp.zeros((DECAY_LORA + ICLR_LORA, 2 * RW), F32)
        wa = wa.at[:DECAY_LORA, :RW].set(w_up[l]).at[DECAY_LORA:, RW:].set(a_up[l]).astype(BF16)
        params = [row(norm1_g[l]), w_in[l].astype(BF16), row(mu_shift[l]), row(w0[l]), row(a0[l]), wa,
                  g_up[l].astype(BF16), row(k_k[l]), row(k_a[l]), row(r_k[l]), row(gn_g[l]), row(gn_b[l]),
                  dw_w[l], row(dw_b[l]), row(cln_g[l]), row(cln_b[l]), ones_bd, ltri, zero, joins]
        mixed = _mixer(x2.reshape(batch, seq, d), params).reshape(batch * seq, RW + CW)
        x2 = _out_ffn(x2, mixed, w_out[l].astype(BF16), row(norm2_g[l]),
                      w_ff1[l].astype(BF16), w_ff2[l].astype(BF16), row(final_g),
                      final=(l == depth - 1))
    return x2.reshape(batch, seq, d)
```
